```python
import jax, jax.numpy as jnp
from jax import lax
import numpy as np

D_MODEL = 1024
BATCH = 32
SEQ = 256
DEPTH = 4
DEC_BATCH = 4
DEC_SEQ = 4096
PAST_LEN = 512

GRID_W = 64
N_AB_LAYERS = (DEPTH + 1) // 2
N_C_LAYERS = DEPTH // 2
EPS = 1e-6

MLA_HEADS = 8
MLA_NOPE = 64
MLA_ROPE = 32
MLA_V = 64
Q_LORA = 256
KV_LORA = 128
ROPE_BASE = 10000.0
QB = 128

RET_HEADS = 4
RET_DK = 128
RET_DV = 128
RET_CHUNK = 128

GM_CHUNK = 128
GM_GROUPS = 8
GM_WIDTH = D_MODEL

D_FF = -(-8 * D_MODEL // (3 * 256)) * 256

AB_SIZES = (Q_LORA, KV_LORA, MLA_ROPE, RET_HEADS * RET_DK, RET_HEADS * RET_DK, RET_HEADS * RET_DV, RET_HEADS * RET_DV)
AB_IN = sum(AB_SIZES)
AB_SPLITS = tuple(int(s) for s in np.cumsum(AB_SIZES)[:-1])
MIX_WIDTH = MLA_HEADS * MLA_V + RET_HEADS * RET_DV

kernel_name = 'hybrid_mla_retention_gmlp_dit_step'


def rmsnorm(x, g):
    xf = x.astype(jnp.float32)
    y = xf * lax.rsqrt(jnp.mean(xf * xf, axis=-1, keepdims=True) + EPS)
    return (y * g.astype(jnp.float32)).astype(x.dtype)


def rms_noaffine(x):
    xf = x.astype(jnp.float32)
    return (xf * lax.rsqrt(jnp.mean(xf * xf, axis=-1, keepdims=True) + EPS)).astype(x.dtype)


def layernorm(x, g, b):
    xf = x.astype(jnp.float32)
    mu = jnp.mean(xf, axis=-1, keepdims=True)
    var = jnp.mean(jnp.square(xf - mu), axis=-1, keepdims=True)
    y = (xf - mu) * lax.rsqrt(var + EPS) * g.astype(jnp.float32) + b.astype(jnp.float32)
    return y.astype(x.dtype)


def modulate(x, shift, scale):
    return x * (1 + scale) + shift


def ada_mod(cond, w, b):
    m = (jax.nn.silu(cond) @ w + b)[:, None, :]
    return jnp.split(m, 6, axis=-1)


def axial_rope_tables(length, dtype):
    rows = length // GRID_W
    row = jnp.repeat(jnp.arange(rows), GRID_W).astype(jnp.float32)
    col = jnp.tile(jnp.arange(GRID_W), rows).astype(jnp.float32)
    half = MLA_ROPE // 2
    inv = 1.0 / jnp.power(ROPE_BASE, jnp.arange(0, half, 2, dtype=jnp.float32) / half)
    ang = jnp.stack([row[:, None] * inv, col[:, None] * inv], axis=1)
    ang = jnp.stack([ang, ang], axis=2).reshape(length, MLA_ROPE)
    return jnp.cos(ang).astype(dtype), jnp.sin(ang).astype(dtype)


def rotate_half_axial(x):
    xs = x.reshape(x.shape[:-1] + (2, 2, MLA_ROPE // 4))
    return jnp.stack([-xs[..., 1, :], xs[..., 0, :]], axis=-2).reshape(x.shape)


def apply_rope(x, cos, sin):
    return x * cos + rotate_half_axial(x) * sin


def to_blocks(t):
    b, l = t.shape[:2]
    return jnp.moveaxis(t.reshape((b, l // QB, QB) + t.shape[2:]), 1, 0)


def from_blocks(t):
    t = jnp.moveaxis(t, 0, 1)
    return t.reshape((t.shape[0], t.shape[1] * t.shape[2]) + t.shape[3:])


def mla_attention(q_nope, q_ropes, key_groups):
    v_all = jnp.concatenate([g[2] for g in key_groups], axis=1)
    scale = (MLA_NOPE + MLA_ROPE) ** -0.5

    def block(args):
        qn, qrs = args
        s = jnp.concatenate(
            [jnp.einsum('bqhd,bkhd->bhqk', qn, kn) + jnp.einsum('bqhr,bkr->bhqk', qr, kr)
             for qr, (kn, kr, _) in zip(qrs, key_groups)], axis=-1)
        p = jax.nn.softmax(s.astype(jnp.float32) * scale, axis=-1).astype(v_all.dtype)
        return jnp.einsum('bhqk,bkhd->bqhd', p, v_all)

    out = lax.map(block, (to_blocks(q_nope), tuple(to_blocks(q) for q in q_ropes)))
    return from_blocks(out)


def retention_scan(q, k, v, log_gamma, s0):
    b, l, h = q.shape[:3]
    n = l // RET_CHUNK

    def chunks(t):
        return jnp.moveaxis(t.reshape(b, n, RET_CHUNK, h, t.shape[3]), 1, 0).transpose(0, 1, 3, 2, 4)

    lg = log_gamma.astype(jnp.float32)
    idx = jnp.arange(RET_CHUNK, dtype=jnp.float32)
    rel = idx[:, None] - idx[None, :]
    dmat = jnp.where(rel >= 0, jnp.exp(lg[:, None, None] * jnp.maximum(rel, 0.0)), 0.0).astype(q.dtype)
    q_decay = jnp.exp(lg[:, None] * (idx + 1.0)).astype(q.dtype)
    k_decay = jnp.exp(lg[:, None] * (RET_CHUNK - 1.0 - idx)).astype(k.dtype)
    chunk_decay = jnp.exp(lg * RET_CHUNK).astype(s0.dtype)

    def step(s, qkv):
        qc, kc, vc = qkv
        inner = jnp.einsum('bhij,bhjd->bhid', jnp.einsum('bhid,bhjd->bhij', qc, kc) * dmat, vc)
        cross = jnp.einsum('bhid,bhde->bhie', qc, s) * q_decay[..., None]
        s_new = s * chunk_decay[:, None, None] + jnp.einsum('bhjd,bhje->bhde', kc * k_decay[..., None], vc)
        return s_new.astype(s.dtype), inner + cross

    s_fin, o = lax.scan(step, s0, (chunks(q), chunks(k), chunks(v)))
    o = o.transpose(1, 0, 3, 2, 4).reshape(b, l, h, v.shape[3])
    return o, s_fin


def bi_retention(q, k, v, log_decay, s0_fwd, s0_bwd):
    lg = -jnp.exp(log_decay.astype(jnp.float32))
    o_f, s_f = retention_scan(q, k, v, lg[0], s0_fwd)
    o_b, s_b = retention_scan(q[:, ::-1], k[:, ::-1], v[:, ::-1], lg[1], s0_bwd)
    return o_f + o_b[:, ::-1], jnp.stack([s_f, s_b], axis=1)


def ab_project(h, w_in, q_norm_g, w_uq, kv_norm_g, w_ukv):
    b, l, _ = h.shape
    cq, ckv, kr, rq, rk, rv, rg = jnp.split(h @ w_in, AB_SPLITS, axis=-1)
    q = (rmsnorm(cq, q_norm_g) @ w_uq).reshape(b, l, MLA_HEADS, MLA_NOPE + MLA_ROPE)
    ckv = rmsnorm(ckv, kv_norm_g)
    kn, v = mla_kv_up(ckv, w_ukv)
    rq = rq.reshape(b, l, RET_HEADS, RET_DK)
    rk = rk.reshape(b, l, RET_HEADS, RET_DK) * (RET_DK ** -0.5)
    rv = rv.reshape(b, l, RET_HEADS, RET_DV)
    return q[..., :MLA_NOPE], q[..., MLA_NOPE:], ckv, kr, kn, v, rq, rk, rv, rg


def mla_kv_up(ckv, w_ukv):
    kv = (ckv @ w_ukv).reshape(ckv.shape[:2] + (MLA_HEADS, MLA_NOPE + MLA_V))
    return kv[..., :MLA_NOPE], kv[..., MLA_NOPE:]


def ab_merge(o_attn, o_ret, rg, w_o):
    b, l = o_attn.shape[:2]
    ret = rms_noaffine(o_ret).reshape(b, l, RET_HEADS * RET_DV) * jax.nn.silu(rg)
    return jnp.concatenate([o_attn.reshape(b, l, MLA_HEADS * MLA_V), ret], axis=-1) @ w_o


def ab_context(h, w_in, q_norm_g, w_uq, kv_norm_g, w_ukv, ret_log_decay, w_o):
    b = h.shape[0]
    qn, qr, ckv, kr, kn, v, rq, rk, rv, rg = ab_project(h, w_in, q_norm_g, w_uq, kv_norm_g, w_ukv)
    o_attn = mla_attention(qn, (qr,), ((kn, kr, v),))
    zeros = jnp.zeros((b, RET_HEADS, RET_DK, RET_DV), h.dtype)
    o_ret, s_ret = bi_retention(rq, rk, rv, ret_log_decay, zeros, zeros)
    return ab_merge(o_attn, o_ret, rg, w_o), ckv, kr, s_ret


def ab_latent(h, ctx_ckv, ctx_kr, ctx_state, w_in, q_norm_g, w_uq, kv_norm_g, w_ukv, ret_log_decay, w_o):
    l = h.shape[1]
    qn, qr, ckv, kr, kn, v, rq, rk, rv, rg = ab_project(h, w_in, q_norm_g, w_uq, kv_norm_g, w_ukv)
    cos, sin = axial_rope_tables(l, h.dtype)
    qr_rot = apply_rope(qr, cos[:, None, :], sin[:, None, :])
    kr_rot = apply_rope(kr, cos, sin)
    kn_c, v_c = mla_kv_up(ctx_ckv, w_ukv)
    o_attn = mla_attention(qn, (qr, qr_rot), ((kn_c, ctx_kr, v_c), (kn, kr_rot, v)))
    o_ret, _ = bi_retention(rq, rk, rv, ret_log_decay, ctx_state[:, 0], ctx_state[:, 1])
    return ab_merge(o_attn, o_ret, rg, w_o)


def chunk_gmlp(h, w_in, ln_g, ln_b, w_s, b_s, w_out):
    b, l, _ = h.shape
    u, v = jnp.split(jax.nn.gelu(h @ w_in), 2, axis=-1)
    v = layernorm(v, ln_g, ln_b)
    v = v.reshape(b, l // GM_CHUNK, GM_CHUNK, GM_GROUPS, GM_WIDTH // GM_GROUPS)
    v = jnp.einsum('gij,bnjgd->bnigd', w_s, v) + b_s.T[:, :, None]
    return (u * v.reshape(b, l, GM_WIDTH)) @ w_out


def swiglu(h, w_gate, w_up, w_down):
    return (jax.nn.silu(h @ w_gate) * (h @ w_up)) @ w_down


def setup_inputs(seed: int = 0) -> dict:
    key = jax.random.key(seed)
    ks = jax.random.split(key, 32)

    def nrm(i, shape, scale=1.0):
        return jax.random.normal(ks[i], shape, jnp.float32) * scale

    decay_base = (-5.0 - jnp.arange(RET_HEADS, dtype=jnp.float32)) * jnp.log(2.0)
    return {
        'x_prompt': nrm(0, (BATCH, SEQ, D_MODEL)),
        'x_sample': nrm(1, (DEC_BATCH, DEC_SEQ, D_MODEL)),
        'cache_mla_ckv': nrm(2, (DEC_BATCH, N_AB_LAYERS, PAST_LEN, KV_LORA)),
        'cache_mla_krope': nrm(3, (DEC_BATCH, N_AB_LAYERS, PAST_LEN, MLA_ROPE)),
        'state_ret': nrm(4, (DEC_BATCH, N_AB_LAYERS, 2, RET_HEADS, RET_DK, RET_DV), 0.5),
        'c': nrm(5, (DEC_BATCH, D_MODEL)),
        'c_ctx': nrm(6, (D_MODEL,)),
        'w_ada': nrm(7, (DEPTH, D_MODEL, 6 * D_MODEL), D_MODEL ** -0.5),
        'b_ada': nrm(8, (DEPTH, 6 * D_MODEL), 0.01),
        'norm_g': 1.0 + nrm(9, (DEPTH, 2, D_MODEL), 0.02),
        'w_in_ab': nrm(10, (N_AB_LAYERS, D_MODEL, AB_IN), D_MODEL ** -0.5),
        'q_norm_g': 1.0 + nrm(11, (N_AB_LAYERS, Q_LORA), 0.02),
        'w_uq': nrm(12, (N_AB_LAYERS, Q_LORA, MLA_HEADS * (MLA_NOPE + MLA_ROPE)), Q_LORA ** -0.5),
        'kv_norm_g': 1.0 + nrm(13, (N_AB_LAYERS, KV_LORA), 0.02),
        'w_ukv': nrm(14, (N_AB_LAYERS, KV_LORA, MLA_HEADS * (MLA_NOPE + MLA_V)), KV_LORA ** -0.5),
        'ret_log_decay': decay_base + nrm(15, (N_AB_LAYERS, 2, RET_HEADS), 0.1),
        'w_o_ab': nrm(16, (N_AB_LAYERS, MIX_WIDTH, D_MODEL), MIX_WIDTH ** -0.5),
        'w_in_c': nrm(17, (N_C_LAYERS, D_MODEL, 2 * GM_WIDTH), D_MODEL ** -0.5),
        'ln_g_c': 1.0 + nrm(18, (N_C_LAYERS, GM_WIDTH), 0.02),
        'ln_b_c': nrm(19, (N_C_LAYERS, GM_WIDTH), 0.01),
        'w_s_c': nrm(20, (N_C_LAYERS, GM_GROUPS, GM_CHUNK, GM_CHUNK), GM_CHUNK ** -0.5),
        'b_s_c': 1.0 + nrm(21, (N_C_LAYERS, GM_GROUPS, GM_CHUNK), 0.02),
        'w_out_c': nrm(22, (N_C_LAYERS, GM_WIDTH, D_MODEL), GM_WIDTH ** -0.5),
        'w_ffn_gate': nrm(23, (DEPTH, D_MODEL, D_FF), D_MODEL ** -0.5),
        'w_ffn_up': nrm(24, (DEPTH, D_MODEL, D_FF), D_MODEL ** -0.5),
        'w_ffn_down': nrm(25, (DEPTH, D_FF, D_MODEL), D_FF ** -0.5),
        'final_norm_g': 1.0 + nrm(26, (D_MODEL,), 0.02),
    }


def reference(x_prompt, x_sample, cache_mla_ckv, cache_mla_krope, state_ret, c, c_ctx,
              w_ada, b_ada, norm_g, w_in_ab, q_norm_g, w_uq, kv_norm_g, w_ukv, ret_log_decay, w_o_ab,
              w_in_c, ln_g_c, ln_b_c, w_s_c, b_s_c, w_out_c, w_ffn_gate, w_ffn_up, w_ffn_down, final_norm_g):
    xp, xs = x_prompt, x_sample
    ckv_list, kr_list, s_list = [], [], []
    for l in range(DEPTH):
        j = l // 2
        sh1p, sc1p, g1p, sh2p, sc2p, g2p = ada_mod(c_ctx[None, :], w_ada[l], b_ada[l])
        sh1s, sc1s, g1s, sh2s, sc2s, g2s = ada_mod(c, w_ada[l], b_ada[l])
        hp = modulate(rmsnorm(xp, norm_g[l, 0]), sh1p, sc1p)
        hs = modulate(rmsnorm(xs, norm_g[l, 0]), sh1s, sc1s)
        if l % 2 == 0:
            yp, ckv_l, kr_l, s_l = ab_context(hp, w_in_ab[j], q_norm_g[j], w_uq[j], kv_norm_g[j], w_ukv[j],
                                              ret_log_decay[j], w_o_ab[j])
            ys = ab_latent(hs, cache_mla_ckv[:, j], cache_mla_krope[:, j], state_ret[:, j],
                           w_in_ab[j], q_norm_g[j], w_uq[j], kv_norm_g[j], w_ukv[j], ret_log_decay[j], w_o_ab[j])
            ckv_list.append(ckv_l)
            kr_list.append(kr_l)
            s_list.append(s_l)
        else:
            yp = chunk_gmlp(hp, w_in_c[j], ln_g_c[j], ln_b_c[j], w_s_c[j], b_s_c[j], w_out_c[j])
            ys = chunk_gmlp(hs, w_in_c[j], ln_g_c[j], ln_b_c[j], w_s_c[j], b_s_c[j], w_out_c[j])
        xp = xp + g1p * yp
        xs = xs + g1s * ys
        hp = modulate(rmsnorm(xp, norm_g[l, 1]), sh2p, sc2p)
        hs = modulate(rmsnorm(xs, norm_g[l, 1]), sh2s, sc2s)
        xp = xp + g2p * swiglu(hp, w_ffn_gate[l], w_ffn_up[l], w_ffn_down[l])
        xs = xs + g2s * swiglu(hs, w_ffn_gate[l], w_ffn_up[l], w_ffn_down[l])
    y_prompt = rmsnorm(xp, final_norm_g)
    y_sample = rmsnorm(xs, final_norm_g)
    new_cache_mla_ckv = jnp.stack(ckv_list, axis=1)
    new_cache_mla_krope = jnp.stack(kr_list, axis=1)
    new_state_ret = jnp.stack(s_list, axis=1)
    return (y_prompt, y_sample, new_cache_mla_ckv, new_cache_mla_krope, new_state_ret)
```

```python
import functools

import jax
import jax.numpy as jnp
from jax import lax
from jax.experimental import pallas as pl
from jax.experimental.pallas import tpu as pltpu

F32 = jnp.float32
BF16 = jnp.bfloat16
EPS = 1e-6

MLA_HEADS = 8
MLA_NOPE = 64
MLA_ROPE = 32
MLA_V = 64
Q_LORA = 256
KV_LORA = 128
ROPE_BASE = 10000.0
GRID_W = 64
RET_HEADS = 4
RET_DK = 128
RET_DV = 128
CHUNK = 128
GM_GROUPS = 8

LANE = 128
ROW_TILE = 512
ATTN_Q_TILE = 256
VMEM_LIMIT = 56 * 1024 * 1024


def _params(n_axes):
    return pltpu.CompilerParams(dimension_semantics=("arbitrary",) * n_axes,
                                vmem_limit_bytes=VMEM_LIMIT)


def _resident(shape):
    nd = len(shape)
    return pl.BlockSpec(shape, lambda *_: (0,) * nd, pipeline_mode=pl.Buffered(1))


def _rms(x):
    return x * lax.rsqrt(jnp.mean(x * x, axis=-1, keepdims=True) + EPS)


def _dot(a, b):
    return jnp.dot(a, b, preferred_element_type=F32)


def _dot_nt(a, b):
    return lax.dot_general(a, b, (((1,), (1,)), ((), ())), preferred_element_type=F32)


def _dot_tn(a, b):
    return lax.dot_general(a, b, (((0,), (0,)), ((), ())), preferred_element_type=F32)


def _tile_heads(x, n):
    return jnp.concatenate([x] * n, axis=1)


def _rotate_half_axial(x):
    xs = x.reshape(x.shape[:-1] + (2, 2, MLA_ROPE // 4))
    return jnp.stack([-xs[..., 1, :], xs[..., 0, :]], axis=-2).reshape(x.shape)


def _ada_kernel(cond_ref, w_ref, b_ref, o_ref):
    c = cond_ref[...]
    a = (c * jax.nn.sigmoid(c)).astype(BF16)
    o_ref[0] = _dot(a, w_ref[0].astype(BF16)) + b_ref[0]


def _ada_mods(cond8, w_ada, b_ada):
    depth, d, n = w_ada.shape
    tn = n // 4
    out = pl.pallas_call(
        _ada_kernel,
        grid=(depth, n // tn),
        in_specs=[pl.BlockSpec((8, d), lambda l, j: (0, 0)),
                  pl.BlockSpec((1, d, tn), lambda l, j: (l, 0, j)),
                  pl.BlockSpec((1, 1, tn), lambda l, j: (l, 0, j))],
        out_specs=pl.BlockSpec((1, 8, tn), lambda l, j: (l, 0, j)),
        out_shape=jax.ShapeDtypeStruct((depth, 8, n), F32),
        compiler_params=_params(2),
        name="ada_mod",
    )(cond8, w_ada, b_ada.reshape(depth, 1, n))
    return out.reshape(depth, 8, 6, d)


class _Tiles:
    def __init__(self, n_prompt, n_sample, dec_seq):
        assert n_prompt % ROW_TILE == 0 and dec_seq % ROW_TILE == 0
        self.tm = ROW_TILE
        self.n_tiles = (n_prompt + n_sample) // ROW_TILE
        self.prompt_tiles = n_prompt // ROW_TILE
        self.tiles_per_seq = dec_seq // ROW_TILE

    def cond(self, i):
        return jnp.where(i < self.prompt_tiles, 0, 1 + (i - self.prompt_tiles) // self.tiles_per_seq)

    def table(self, i):
        return jnp.where(i < self.prompt_tiles, 0, 1 + (i - self.prompt_tiles) % self.tiles_per_seq)

    def rows(self, width):
        return pl.BlockSpec((self.tm, width), lambda i: (i, 0))

    def mods(self, layer, d):
        return pl.BlockSpec((1, 1, 6, d), lambda i: (layer, self.cond(i), 0, 0))


def _modulated(x, g, shift, scale):
    return (_rms(x) * g) * (1.0 + scale) + shift


def _ffn_kernel(x_ref, mod_ref, g_ref, wg_ref, wu_ref, wd_ref, fg_ref, o_ref, *, final_norm):
    x = x_ref[...]
    hb = _modulated(x, g_ref[...], mod_ref[0, 0, 3:4, :], mod_ref[0, 0, 4:5, :]).astype(BF16)
    gate = _dot(hb, wg_ref[...])
    up = _dot(hb, wu_ref[...])
    act = (gate * jax.nn.sigmoid(gate) * up).astype(BF16)
    y = x + mod_ref[0, 0, 5:6, :] * _dot(act, wd_ref[...])
    if final_norm:
        y = _rms(y) * fg_ref[...]
    o_ref[...] = y


def _ffn(tiles, layer, x, mods, g, wg, wu, wd, fg, final_norm):
    t, d = x.shape
    f = wg.shape[1]
    return pl.pallas_call(
        functools.partial(_ffn_kernel, final_norm=final_norm),
        grid=(tiles.n_tiles,),
        in_specs=[tiles.rows(d), tiles.mods(layer, d), _resident((1, d)),
                  _resident((d, f)), _resident((d, f)), _resident((f, d)), _resident((1, d))],
        out_specs=tiles.rows(d),
        out_shape=jax.ShapeDtypeStruct((t, d), F32),
        compiler_params=_params(1),
        name=f"ffn_{layer}",
    )(x, mods, g, wg, wu, wd, fg)


def _cmix_kernel(x_ref, mod_ref, g_ref, win_ref, lng_ref, lnb_ref, ws_ref, bs_ref, wout_ref, o_ref, gated_ref):
    x = x_ref[...]
    tm = x.shape[0]
    w = lng_ref.shape[1]
    hb = _modulated(x, g_ref[...], mod_ref[0, 0, 0:1, :], mod_ref[0, 0, 1:2, :]).astype(BF16)
    uv = jax.nn.gelu(_dot(hb, win_ref[...]))
    u = uv[:, :w]
    v = uv[:, w:]
    mu = jnp.mean(v, axis=-1, keepdims=True)
    vc = v - mu
    var = jnp.mean(vc * vc, axis=-1, keepdims=True)
    vb = (vc * lax.rsqrt(var + EPS) * lng_ref[...] + lnb_ref[...]).astype(BF16)
    gw = w // GM_GROUPS
    for r in range(tm // CHUNK):
        rs = slice(r * CHUNK, (r + 1) * CHUNK)
        for gi in range(GM_GROUPS):
            cs = slice(gi * gw, (gi + 1) * gw)
            mixed = _dot(ws_ref[gi], vb[rs, cs]) + bs_ref[gi]
            gated_ref[rs, cs] = (u[rs, cs] * mixed).astype(BF16)
    o_ref[...] = x + mod_ref[0, 0, 2:3, :] * _dot(gated_ref[...], wout_ref[...])


def _cmix(tiles, layer, x, mods, g, win, lng, lnb, ws, bs, wout):
    t, d = x.shape
    w = wout.shape[0]
    return pl.pallas_call(
        _cmix_kernel,
        grid=(tiles.n_tiles,),
        in_specs=[tiles.rows(d), tiles.mods(layer, d), _resident((1, d)), _resident((d, 2 * w)),
                  _resident((1, w)), _resident((1, w)), _resident(ws.shape), _resident(bs.shape),
                  _resident((w, d))],
        out_specs=tiles.rows(d),
        out_shape=jax.ShapeDtypeStruct((t, d), F32),
        scratch_shapes=[pltpu.VMEM((tiles.tm, w), BF16)],
        compiler_params=_params(1),
        name=f"cmix_{layer}",
    )(x, mods, g, win, lng, lnb, ws, bs, wout)


_HW = MLA_HEADS * LANE
_RW = RET_HEADS * RET_DK
_C_CQ = 0
_C_CKV = _C_CQ + Q_LORA
_C_RQ = _C_CKV + KV_LORA
_C_RK = _C_RQ + _RW
_C_RV = _C_RK + _RW
_C_RG = _C_RV + _RW
_C_KRA = _C_RG + _RW
_C_KRB = _C_KRA + LANE
_C_END = _C_KRB + LANE


def _abproj_kernel(x_ref, mod_ref, g_ref, tab_ref, win_ref, qg_ref, wuq_ref, kvg_ref, wukv_ref,
                   q_ref, k_ref, v_ref, rq_ref, rk_ref, rv_ref, rg_ref, ckv_ref, kr_ref):
    x = x_ref[...]
    hb = _modulated(x, g_ref[...], mod_ref[0, 0, 0:1, :], mod_ref[0, 0, 1:2, :]).astype(BF16)
    p = _dot(hb, win_ref[...])
    rq_ref[...] = p[:, _C_RQ:_C_RK].astype(BF16)
    rk_ref[...] = (p[:, _C_RK:_C_RV] * (RET_DK ** -0.5)).astype(BF16)
    rv_ref[...] = p[:, _C_RV:_C_RG].astype(BF16)
    rg_ref[...] = p[:, _C_RG:_C_KRA]
    kr_a = p[:, _C_KRA:_C_KRB]
    kr_b = p[:, _C_KRB:_C_END]
    kr_ref[...] = kr_a

    tab = tab_ref[...]
    cos_q, sin_q = tab[:, 0:LANE], tab[:, LANE:2 * LANE]
    cos_k, sin_k = tab[:, 2 * LANE:3 * LANE], tab[:, 3 * LANE:4 * LANE]

    cqn = (_rms(p[:, _C_CQ:_C_CKV]) * qg_ref[...]).astype(BF16)
    q2 = _dot(cqn, wuq_ref[...])
    q = q2[:, :_HW] * _tile_heads(cos_q, MLA_HEADS) + q2[:, _HW:] * _tile_heads(sin_q, MLA_HEADS)
    q_ref[...] = q.astype(BF16)

    ckv = _rms(p[:, _C_CKV:_C_RQ]) * kvg_ref[...]
    ckv_ref[...] = ckv
    kv = _dot(ckv.astype(BF16), wukv_ref[...])
    k_rope = kr_a * cos_k + kr_b * sin_k
    k_ref[...] = (kv[:, :_HW] + _tile_heads(k_rope, MLA_HEADS)).astype(BF16)
    v_ref[...] = kv[:, _HW:].astype(BF16)


def _abproj(tiles, layer, x, mods, g, table, win, qg, wuq, kvg, wukv):
    t, d = x.shape
    out_widths = [(_HW, BF16), (_HW, BF16), (_HW, BF16), (_RW, BF16), (_RW, BF16), (_RW, BF16),
                  (_RW, F32), (KV_LORA, F32), (LANE, F32)]
    return pl.pallas_call(
        _abproj_kernel,
        grid=(tiles.n_tiles,),
        in_specs=[tiles.rows(d), tiles.mods(layer, d), _resident((1, d)),
                  pl.BlockSpec((tiles.tm, 4 * LANE), lambda i: (tiles.table(i), 0)),
                  _resident(win.shape), _resident(qg.shape), _resident(wuq.shape),
                  _resident(kvg.shape), _resident(wukv.shape)],
        out_specs=[tiles.rows(wd) for wd, _ in out_widths],
        out_shape=[jax.ShapeDtypeStruct((t, wd), dt) for wd, dt in out_widths],
        compiler_params=_params(1),
        name=f"abproj_{layer}",
    )(x, mods, g, table, win, qg, wuq, kvg, wukv)


def _ctxkv_kernel(ckv_ref, kr_ref, wukv_ref, k_ref, v_ref):
    kv = _dot(ckv_ref[...].astype(BF16), wukv_ref[...])
    k_ref[...] = (kv[:, :_HW] + _tile_heads(kr_ref[...], MLA_HEADS)).astype(BF16)
    v_ref[...] = kv[:, _HW:].astype(BF16)


def _ctxkv(layer_idx, cache_ckv, cache_kr_padded, wukv):
    b, _, past, kvl = cache_ckv.shape
    return pl.pallas_call(
        _ctxkv_kernel,
        grid=(b,),
        in_specs=[pl.BlockSpec((None, None, past, kvl), lambda i: (i, layer_idx, 0, 0)),
                  pl.BlockSpec((None, None, past, LANE), lambda i: (i, layer_idx, 0, 0)),
                  _resident(wukv.shape)],
        out_specs=[pl.BlockSpec((past, _HW), lambda i: (i, 0))] * 2,
        out_shape=[jax.ShapeDtypeStruct((b * past, _HW), BF16)] * 2,
        compiler_params=_params(1),
        name=f"ctxkv_{layer_idx}",
    )(cache_ckv, cache_kr_padded, wukv)


_ATTN_SCALE = (MLA_NOPE + MLA_ROPE) ** -0.5


def _attn_kernel(*refs, has_ctx):
    if has_ctx:
        q_ref, k_ref, v_ref, kc_ref, vc_ref, o_ref = refs
    else:
        q_ref, k_ref, v_ref, o_ref = refs
    for pair in range(MLA_HEADS // 2):
        o_pair = None
        for sub in range(2):
            hs = slice((2 * pair + sub) * LANE, (2 * pair + sub + 1) * LANE)
            q = q_ref[:, hs]
            s = _dot_nt(q, k_ref[:, hs]) * _ATTN_SCALE
            m = jnp.max(s, axis=-1, keepdims=True)
            if has_ctx:
                sc = _dot_nt(q, kc_ref[:, hs]) * _ATTN_SCALE
                m = jnp.maximum(m, jnp.max(sc, axis=-1, keepdims=True))
            p = jnp.exp(s - m)
            l = jnp.sum(p, axis=-1, keepdims=True)
            pv = _dot(p.astype(BF16), v_ref[:, hs])
            if has_ctx:
                pc = jnp.exp(sc - m)
                l = l + jnp.sum(pc, axis=-1, keepdims=True)
                pv = pv + _dot(pc.astype(BF16), vc_ref[:, hs])
            contrib = pv * (1.0 / l)
            o_pair = contrib if o_pair is None else o_pair + contrib
        o_ref[:, pair * LANE:(pair + 1) * LANE] = o_pair.astype(o_ref.dtype)


def _attention(name, q, k, v, n_seq, seq_len, row_off, ctx=None):
    tq = ATTN_Q_TILE
    qt = seq_len // tq
    q_off, s_off = row_off // tq, row_off // seq_len
    assert row_off % seq_len == 0
    ow = MLA_HEADS * MLA_V
    seq_spec = pl.BlockSpec((seq_len, _HW), lambda b, i: (b + s_off, 0), pipeline_mode=pl.Buffered(1))
    in_specs = [pl.BlockSpec((tq, _HW), lambda b, i: (b * qt + i + q_off, 0)), seq_spec, seq_spec]
    args = [q, k, v]
    if ctx is not None:
        kc, vc, past = ctx
        ctx_spec = pl.BlockSpec((past, _HW), lambda b, i: (b, 0), pipeline_mode=pl.Buffered(1))
        in_specs += [ctx_spec, ctx_spec]
        args += [kc, vc]
    return pl.pallas_call(
        functools.partial(_attn_kernel, has_ctx=ctx is not None),
        grid=(n_seq, qt),
        in_specs=in_specs,
        out_specs=pl.BlockSpec((tq, ow), lambda b, i: (b * qt + i, 0)),
        out_shape=jax.ShapeDtypeStruct((n_seq * seq_len, ow), BF16),
        compiler_params=_params(2),
        name=name,
    )(*args)


_RET_PAIR = 2
_RET_PW = _RET_PAIR * RET_DK


def _ret_kernel(*refs, n_chunks, has_s0, want_final):
    refs = list(refs)
    ld_ref, q_ref, k_ref, v_ref, g_ref = refs[:5]
    refs = refs[5:]
    s0_ref = refs.pop(0) if has_s0 else None
    o_ref = refs.pop(0)
    sfin_ref = refs.pop(0) if want_final else None
    cross_ref, sf_ref, sb_ref = refs

    ii = lax.broadcasted_iota(jnp.int32, (CHUNK, CHUNK), 0).astype(F32)
    jj = lax.broadcasted_iota(jnp.int32, (CHUNK, CHUNK), 1).astype(F32)
    rel = ii - jj
    consts = []
    for h in range(_RET_PAIR):
        lg_f = -jnp.exp(ld_ref[0, h])
        lg_b = -jnp.exp(ld_ref[1, h])
        mask = (jnp.where(rel >= 0, jnp.exp(lg_f * jnp.maximum(rel, 0.0)), 0.0)
                + jnp.where(rel <= 0, jnp.exp(lg_b * jnp.maximum(-rel, 0.0)), 0.0))
        consts.append(dict(
            mask=mask,
            qd_f=jnp.exp(lg_f * (ii + 1.0)), kd_f=jnp.exp(lg_f * (CHUNK - 1.0 - ii)), cd_f=jnp.exp(lg_f * CHUNK),
            qd_b=jnp.exp(lg_b * (CHUNK - ii)), kd_b=jnp.exp(lg_b * ii), cd_b=jnp.exp(lg_b * CHUNK)))
        if has_s0:
            sf_ref[h] = s0_ref[0, 0, h]
            sb_ref[h] = s0_ref[0, 1, h]
        else:
            sf_ref[h] = jnp.zeros((RET_DK, RET_DV), F32)
            sb_ref[h] = jnp.zeros((RET_DK, RET_DV), F32)

    def chunk_rows(n):
        return pl.ds(pl.multiple_of(n * CHUNK, CHUNK), CHUNK)

    def bwd_step(t, carry):
        rows = chunk_rows(n_chunks - 1 - t)
        for h in range(_RET_PAIR):
            c = consts[h]
            hs = slice(h * RET_DK, (h + 1) * RET_DK)
            q, k, v = q_ref[rows, hs], k_ref[rows, hs], v_ref[rows, hs]
            s = sb_ref[h]
            cross_ref[rows, hs] = _dot(q, s.astype(BF16)) * c["qd_b"]
            sb_ref[h] = s * c["cd_b"] + _dot_tn((k.astype(F32) * c["kd_b"]).astype(BF16), v)
        return carry

    def fwd_step(n, carry):
        rows = chunk_rows(n)
        for h in range(_RET_PAIR):
            c = consts[h]
            hs = slice(h * RET_DK, (h + 1) * RET_DK)
            q, k, v = q_ref[rows, hs], k_ref[rows, hs], v_ref[rows, hs]
            s = sf_ref[h]
            inner = _dot((_dot_nt(q, k) * c["mask"]).astype(BF16), v)
            o = inner + _dot(q, s.astype(BF16)) * c["qd_f"] + cross_ref[rows, hs]
            gate = g_ref[rows, hs]
            o_ref[rows, hs] = (_rms(o) * (gate * jax.nn.sigmoid(gate))).astype(o_ref.dtype)
            sf_ref[h] = s * c["cd_f"] + _dot_tn((k.astype(F32) * c["kd_f"]).astype(BF16), v)
        return carry

    lax.fori_loop(0, n_chunks, bwd_step, 0)
    lax.fori_loop(0, n_chunks, fwd_step, 0)
    if want_final:
        for h in range(_RET_PAIR):
            sfin_ref[0, 0, h] = sf_ref[h]
            sfin_ref[0, 1, h] = sb_ref[h]


def _retention(name, ld, rq, rk, rv, rg, n_seq, seq_len, row_off, s0=None, want_final=False):
    s_off = row_off // seq_len
    assert row_off % seq_len == 0
    pairs = RET_HEADS // _RET_PAIR
    seq = pl.BlockSpec((seq_len, _RET_PW), lambda b, p: (b + s_off, p))
    state = pl.BlockSpec((1, 2, _RET_PAIR, RET_DK, RET_DV), lambda b, p: (b, 0, p, 0, 0))
    in_specs = [pl.BlockSpec((2, _RET_PAIR, CHUNK, CHUNK), lambda b, p: (0, p, 0, 0)), seq, seq, seq, seq]
    args = [ld, rq, rk, rv, rg]
    if s0 is not None:
        in_specs.append(state)
        args.append(s0)
    out_specs = [pl.BlockSpec((seq_len, _RET_PW), lambda b, p: (b, p))]
    out_shape = [jax.ShapeDtypeStruct((n_seq * seq_len, _RW), BF16)]
    if want_final:
        out_specs.append(state)
        out_shape.append(jax.ShapeDtypeStruct((n_seq, 2, RET_HEADS, RET_DK, RET_DV), F32))
    res = pl.pallas_call(
        functools.partial(_ret_kernel, n_chunks=seq_len // CHUNK, has_s0=s0 is not None, want_final=want_final),
        grid=(n_seq, pairs),
        in_specs=in_specs,
        out_specs=out_specs,
        out_shape=out_shape,
        scratch_shapes=[pltpu.VMEM((seq_len, _RET_PW), F32),
                        pltpu.VMEM((_RET_PAIR, RET_DK, RET_DV), F32),
                        pltpu.VMEM((_RET_PAIR, RET_DK, RET_DV), F32)],
        compiler_params=_params(2),
        name=name,
    )(*args)
    return res if want_final else res[0]


def _merge_kernel(x_ref, mod_ref, oa_ref, ret_ref, woa_ref, wor_ref, o_ref):
    y = _dot(oa_ref[...], woa_ref[...]) + _dot(ret_ref[...], wor_ref[...])
    o_ref[...] = x_ref[...] + mod_ref[0, 0, 2:3, :] * y


def _merge(tiles, layer, x, mods, o_attn, ret, woa, wor):
    t, d = x.shape
    return pl.pallas_call(
        _merge_kernel,
        grid=(tiles.n_tiles,),
        in_specs=[tiles.rows(d), tiles.mods(layer, d), tiles.rows(o_attn.shape[1]), tiles.rows(ret.shape[1]),
                  _resident(woa.shape), _resident(wor.shape)],
        out_specs=tiles.rows(d),
        out_shape=jax.ShapeDtypeStruct((t, d), F32),
        compiler_params=_params(1),
        name=f"merge_{layer}",
    )(x, mods, o_attn, ret, woa, wor)


def _rope_table(dec_seq, tm):
    rows = dec_seq // GRID_W
    row = jnp.repeat(jnp.arange(rows), GRID_W).astype(F32)
    col = jnp.tile(jnp.arange(GRID_W), rows).astype(F32)
    half = MLA_ROPE // 2
    inv = 1.0 / jnp.power(ROPE_BASE, jnp.arange(0, half, 2, dtype=F32) / half)
    ang = jnp.stack([row[:, None] * inv, col[:, None] * inv], axis=1)
    ang = jnp.stack([ang, ang], axis=2).reshape(dec_seq, MLA_ROPE)
    cos, sin = jnp.cos(ang), jnp.sin(ang)
    lo = MLA_NOPE + MLA_ROPE
    ones, zeros = jnp.ones((dec_seq, lo), F32), jnp.zeros((dec_seq, lo), F32)
    pos = jnp.concatenate([ones, cos, zeros, sin, zeros, cos, zeros, sin], axis=1)
    flat_k = jnp.concatenate([jnp.zeros((tm, MLA_NOPE), F32), jnp.ones((tm, MLA_ROPE), F32),
                              jnp.zeros((tm, MLA_ROPE), F32)], axis=1)
    flat = jnp.concatenate([jnp.ones((tm, LANE), F32), jnp.zeros((tm, LANE), F32), flat_k,
                            jnp.zeros((tm, LANE), F32)], axis=1)
    return jnp.concatenate([flat, pos], axis=0)


def _layout_w_in(w):
    d = w.shape[0]
    o = 0
    parts = {}
    for name, width in (("cq", Q_LORA), ("ckv", KV_LORA), ("kr", MLA_ROPE), ("rq", _RW), ("rk", _RW),
                        ("rv", _RW), ("rg", _RW)):
        parts[name] = w[:, o:o + width]
        o += width
    kr = parts["kr"]
    kr_a = jnp.concatenate([jnp.zeros((d, MLA_NOPE), w.dtype), kr, kr], axis=1)
    kr_b = jnp.concatenate([jnp.zeros((d, MLA_NOPE + MLA_ROPE), w.dtype), _rotate_half_axial(kr)], axis=1)
    return jnp.concatenate([parts["cq"], parts["ckv"], parts["rq"], parts["rk"], parts["rv"], parts["rg"],
                            kr_a, kr_b], axis=1).astype(BF16)


def _layout_w_uq(w):
    r = w.shape[0]
    wh = w.reshape(r, MLA_HEADS, MLA_NOPE + MLA_ROPE)
    nope, rope = wh[..., :MLA_NOPE], wh[..., MLA_NOPE:]
    main = jnp.concatenate([nope, rope, rope], axis=-1).reshape(r, _HW)
    rot = jnp.concatenate([jnp.zeros(nope.shape[:2] + (MLA_NOPE + MLA_ROPE,), w.dtype),
                           _rotate_half_axial(rope)], axis=-1).reshape(r, _HW)
    return jnp.concatenate([main, rot], axis=1).astype(BF16)


def _layout_w_ukv(w):
    r = w.shape[0]
    wh = w.reshape(r, MLA_HEADS, MLA_NOPE + MLA_V)
    kn, v = wh[..., :MLA_NOPE], wh[..., MLA_NOPE:]
    k_main = jnp.concatenate([kn, jnp.zeros_like(kn)], axis=-1).reshape(r, _HW)
    vp = v.reshape(r, MLA_HEADS // 2, 2, MLA_V)
    z = jnp.zeros_like(vp[:, :, 0])
    v_even = jnp.concatenate([vp[:, :, 0], z], axis=-1)
    v_odd = jnp.concatenate([z, vp[:, :, 1]], axis=-1)
    v_exp = jnp.stack([v_even, v_odd], axis=2).reshape(r, _HW)
    return jnp.concatenate([k_main, v_exp], axis=1).astype(BF16)


def kernel(x_prompt, x_sample, cache_mla_ckv, cache_mla_krope, state_ret, c, c_ctx, w_ada, b_ada, norm_g, w_in_ab, q_norm_g, w_uq, kv_norm_g, w_ukv, ret_log_decay, w_o_ab, w_in_c, ln_g_c, ln_b_c, w_s_c, b_s_c, w_out_c, w_ffn_gate, w_ffn_up, w_ffn_down, final_norm_g):
    batch, seq, d = x_prompt.shape
    dec_batch, dec_seq, _ = x_sample.shape
    depth = w_ada.shape[0]
    past = cache_mla_ckv.shape[2]
    n_prompt, n_sample = batch * seq, dec_batch * dec_seq
    tiles = _Tiles(n_prompt, n_sample, dec_seq)

    cond8 = jnp.zeros((8, d), F32).at[0].set(c_ctx).at[1:1 + dec_batch].set(c)
    mods = _ada_mods(cond8, w_ada, b_ada)

    x = jnp.concatenate([x_prompt.reshape(n_prompt, d), x_sample.reshape(n_sample, d)], axis=0)
    table = _rope_table(dec_seq, tiles.tm)
    cache_kr_padded = jnp.pad(cache_mla_krope, ((0, 0), (0, 0), (0, 0), (MLA_NOPE, LANE - MLA_NOPE - MLA_ROPE)))
    fg = final_norm_g.reshape(1, d)

    ckv_out, kr_out, state_out = [], [], []
    for l in range(depth):
        j = l // 2
        g1 = norm_g[l, 0].reshape(1, d)
        g2 = norm_g[l, 1].reshape(1, d)
        if l % 2 == 0:
            wukv = _layout_w_ukv(w_ukv[j])
            (q, k, v, rq, rk, rv, rg, ckv, kr) = _abproj(
                tiles, l, x, mods, g1, table, _layout_w_in(w_in_ab[j]), q_norm_g[j].reshape(1, -1),
                _layout_w_uq(w_uq[j]), kv_norm_g[j].reshape(1, -1), wukv)
            kc, vc = _ctxkv(j, cache_mla_ckv, cache_kr_padded, wukv)
            oa_p = _attention(f"attn_prompt_{l}", q, k, v, batch, seq, 0)
            oa_s = _attention(f"attn_sample_{l}", q, k, v, dec_batch, dec_seq, n_prompt, ctx=(kc, vc, past))
            ld = jnp.broadcast_to(ret_log_decay[j][:, :, None, None], (2, RET_HEADS, CHUNK, CHUNK))
            ret_p, s_fin = _retention(f"ret_prompt_{l}", ld, rq, rk, rv, rg, batch, seq, 0, want_final=True)
            ret_s = _retention(f"ret_sample_{l}", ld, rq, rk, rv, rg, dec_batch, dec_seq, n_prompt,
                               s0=state_ret[:, j])
            wo = w_o_ab[j].astype(BF16)
            aw = MLA_HEADS * MLA_V
            x = _merge(tiles, l, x, mods, jnp.concatenate([oa_p, oa_s], axis=0),
                       jnp.concatenate([ret_p, ret_s], axis=0), wo[:aw], wo[aw:])
            ckv_out.append(ckv[:n_prompt].reshape(batch, seq, KV_LORA))
            kr_out.append(kr[:n_prompt, MLA_NOPE:MLA_NOPE + MLA_ROPE].reshape(batch, seq, MLA_ROPE))
            state_out.append(s_fin)
        else:
            gw = w_out_c.shape[1] // GM_GROUPS
            bs = jnp.broadcast_to(b_s_c[j][:, :, None], (GM_GROUPS, CHUNK, gw))
            x = _cmix(tiles, l, x, mods, g1, w_in_c[j].astype(BF16), ln_g_c[j].reshape(1, -1),
                      ln_b_c[j].reshape(1, -1), w_s_c[j].astype(BF16), bs, w_out_c[j].astype(BF16))
        x = _ffn(tiles, l, x, mods, g2, w_ffn_gate[l].astype(BF16), w_ffn_up[l].astype(BF16),
                 w_ffn_down[l].astype(BF16), fg, final_norm=(l == depth - 1))

    y_prompt = x[:n_prompt].reshape(batch, seq, d)
    y_sample = x[n_prompt:].reshape(dec_batch, dec_seq, d)
    return (y_prompt, y_sample, jnp.stack(ckv_out, axis=1), jnp.stack(kr_out, axis=1),
            jnp.stack(state_out, axis=1))
```

```python
import functools
import math

import jax
import jax.numpy as jnp
from jax import lax
from jax.experimental import pallas as pl
from jax.experimental.pallas import tpu as pltpu

F32 = jnp.float32
BF16 = jnp.bfloat16
EPS = 1e-6

MLA_HEADS = 8
MLA_NOPE = 64
MLA_ROPE = 32
MLA_V = 64
Q_LORA = 256
KV_LORA = 128
ROPE_BASE = 10000.0
GRID_W = 64
RET_HEADS = 4
RET_DK = 128
RET_DV = 128
CHUNK = 128
GM_GROUPS = 8

LANE = 128
ROW_TILE = 512
ATTN_Q_TILE = 512
KEY_CHUNK = 256
VMEM_LIMIT = 56 * 1024 * 1024


def _params(n_axes):
    return pltpu.CompilerParams(dimension_semantics=("arbitrary",) * n_axes,
                                vmem_limit_bytes=VMEM_LIMIT)


def _resident(shape):
    nd = len(shape)
    return pl.BlockSpec(shape, lambda *_: (0,) * nd, pipeline_mode=pl.Buffered(1))


def _rms(x):
    return x * lax.rsqrt(jnp.mean(x * x, axis=-1, keepdims=True) + EPS)


def _dot(a, b):
    return jnp.dot(a, b, preferred_element_type=F32)


def _dot_nt(a, b):
    return lax.dot_general(a, b, (((1,), (1,)), ((), ())), preferred_element_type=F32)


def _dot_tn(a, b):
    return lax.dot_general(a, b, (((0,), (0,)), ((), ())), preferred_element_type=F32)


def _tile_heads(x, n):
    return jnp.concatenate([x] * n, axis=1)


def _rotate_half_axial(x):
    xs = x.reshape(x.shape[:-1] + (2, 2, MLA_ROPE // 4))
    return jnp.stack([-xs[..., 1, :], xs[..., 0, :]], axis=-2).reshape(x.shape)


def _ada_kernel(cond_ref, w_ref, b_ref, o_ref):
    c = cond_ref[...]
    a = (c * jax.nn.sigmoid(c)).astype(BF16)
    o_ref[0] = _dot(a, w_ref[0].astype(BF16)) + b_ref[0]


def _ada_mods(cond8, w_ada, b_ada):
    depth, d, n = w_ada.shape
    tn = n // 4
    out = pl.pallas_call(
        _ada_kernel,
        grid=(depth, n // tn),
        in_specs=[pl.BlockSpec((8, d), lambda l, j: (0, 0)),
                  pl.BlockSpec((1, d, tn), lambda l, j: (l, 0, j)),
                  pl.BlockSpec((1, 1, tn), lambda l, j: (l, 0, j))],
        out_specs=pl.BlockSpec((1, 8, tn), lambda l, j: (l, 0, j)),
        out_shape=jax.ShapeDtypeStruct((depth, 8, n), F32),
        compiler_params=_params(2),
        name="ada_mod",
    )(cond8, w_ada, b_ada.reshape(depth, 1, n))
    return out.reshape(depth, 8, 6, d)


class _Group:
    def __init__(self, name, n_seq, seq_len, cond_base, positional):
        assert seq_len % ROW_TILE == 0 or ROW_TILE % seq_len == 0
        self.name = name
        self.n_seq, self.seq_len = n_seq, seq_len
        self.n_rows = n_seq * seq_len
        self.tm = ROW_TILE
        self.n_tiles = self.n_rows // ROW_TILE
        self.cond_base = cond_base
        self.tiles_per_cond = seq_len // ROW_TILE if positional else self.n_tiles
        self.positional = positional

    def cond(self, i):
        return self.cond_base + i // self.tiles_per_cond

    def table(self, i):
        return 1 + i % self.tiles_per_cond if self.positional else 0

    def rows(self, width):
        return pl.BlockSpec((self.tm, width), lambda i: (i, 0))

    def cols(self, height):
        return pl.BlockSpec((height, self.tm), lambda i: (0, i))

    def mods(self, layer, d):
        return pl.BlockSpec((1, 1, 6, d), lambda i: (layer, self.cond(i), 0, 0))


def _modulated(x, g, shift, scale):
    return (_rms(x) * g) * (1.0 + scale) + shift


def _swiglu_residual(x, mod_ref, g_ref, wg_ref, wu_ref, wd_ref):
    hb = _modulated(x, g_ref[...], mod_ref[0, 0, 3:4, :], mod_ref[0, 0, 4:5, :]).astype(BF16)
    gate = _dot(hb, wg_ref[...])
    up = _dot(hb, wu_ref[...])
    act = (gate * jax.nn.sigmoid(gate) * up).astype(BF16)
    return x + mod_ref[0, 0, 5:6, :] * _dot(act, wd_ref[...])


def _ffn_kernel(x_ref, mod_ref, g_ref, wg_ref, wu_ref, wd_ref, fg_ref, o_ref, *, final_norm):
    y = _swiglu_residual(x_ref[...], mod_ref, g_ref, wg_ref, wu_ref, wd_ref)
    if final_norm:
        y = _rms(y) * fg_ref[...]
    o_ref[...] = y


def _ffn(grp, layer, x, mods, g, wg, wu, wd, fg, final_norm):
    t, d = x.shape
    f = wg.shape[1]
    return pl.pallas_call(
        functools.partial(_ffn_kernel, final_norm=final_norm),
        grid=(grp.n_tiles,),
        in_specs=[grp.rows(d), grp.mods(layer, d), _resident((1, d)),
                  _resident((d, f)), _resident((d, f)), _resident((f, d)), _resident((1, d))],
        out_specs=grp.rows(d),
        out_shape=jax.ShapeDtypeStruct((t, d), F32),
        compiler_params=_params(1),
        name=f"ffn_{grp.name}_{layer}",
    )(x, mods, g, wg, wu, wd, fg)


def _cmix_kernel(x_ref, mod_ref, g_ref, win_ref, lng_ref, lnb_ref, ws_ref, bs_ref, wout_ref, o_ref, gated_ref):
    x = x_ref[...]
    tm = x.shape[0]
    w = lng_ref.shape[1]
    hb = _modulated(x, g_ref[...], mod_ref[0, 0, 0:1, :], mod_ref[0, 0, 1:2, :]).astype(BF16)
    uv = jax.nn.gelu(_dot(hb, win_ref[...]))
    u = uv[:, :w]
    v = uv[:, w:]
    mu = jnp.mean(v, axis=-1, keepdims=True)
    vc = v - mu
    var = jnp.mean(vc * vc, axis=-1, keepdims=True)
    vb = (vc * lax.rsqrt(var + EPS) * lng_ref[...] + lnb_ref[...]).astype(BF16)
    gw = w // GM_GROUPS
    for r in range(tm // CHUNK):
        rs = slice(r * CHUNK, (r + 1) * CHUNK)
        for gi in range(GM_GROUPS):
            cs = slice(gi * gw, (gi + 1) * gw)
            mixed = _dot(ws_ref[gi], vb[rs, cs]) + bs_ref[gi]
            gated_ref[rs, cs] = (u[rs, cs] * mixed).astype(BF16)
    o_ref[...] = x + mod_ref[0, 0, 2:3, :] * _dot(gated_ref[...], wout_ref[...])


def _cmix(grp, layer, x, mods, g, win, lng, lnb, ws, bs, wout):
    t, d = x.shape
    w = wout.shape[0]
    return pl.pallas_call(
        _cmix_kernel,
        grid=(grp.n_tiles,),
        in_specs=[grp.rows(d), grp.mods(layer, d), _resident((1, d)), _resident((d, 2 * w)),
                  _resident((1, w)), _resident((1, w)), _resident(ws.shape), _resident(bs.shape),
                  _resident((w, d))],
        out_specs=grp.rows(d),
        out_shape=jax.ShapeDtypeStruct((t, d), F32),
        scratch_shapes=[pltpu.VMEM((grp.tm, w), BF16)],
        compiler_params=_params(1),
        name=f"cmix_{grp.name}_{layer}",
    )(x, mods, g, win, lng, lnb, ws, bs, wout)


_HW = MLA_HEADS * LANE
_RW = RET_HEADS * RET_DK
_C_CQ = 0
_C_CKV = _C_CQ + Q_LORA
_C_RQ = _C_CKV + KV_LORA
_C_RK = _C_RQ + _RW
_C_RV = _C_RK + _RW
_C_RG = _C_RV + _RW
_C_KRA = _C_RG + _RW
_C_KRB = _C_KRA + LANE
_C_END = _C_KRB + LANE

_Q_SCALE = (MLA_NOPE + MLA_ROPE) ** -0.5 * math.log2(math.e)


def _value_rows_t(ckv_b, wvt_ref):
    vt = _dot_nt(wvt_ref[...], ckv_b)
    row = lax.broadcasted_iota(jnp.int32, vt.shape, 0)
    return jnp.where((row & (LANE - 1)) == MLA_V, 1.0, vt)


def _abproj_kernel(*refs, with_cache):
    (x_ref, mod_ref, g_ref, tab_ref, win_ref, qg_ref, wuq_ref, kvg_ref, wuk_ref, wvt_ref,
     q_ref, k_ref, vt_ref, rq_ref, rk_ref, rv_ref, rg_ref) = refs[:17]
    x = x_ref[...]
    hb = _modulated(x, g_ref[...], mod_ref[0, 0, 0:1, :], mod_ref[0, 0, 1:2, :]).astype(BF16)
    p = _dot(hb, win_ref[...])
    rq_ref[...] = p[:, _C_RQ:_C_RK].astype(BF16)
    rk_ref[...] = (p[:, _C_RK:_C_RV] * (RET_DK ** -0.5)).astype(BF16)
    rv_ref[...] = p[:, _C_RV:_C_RG].astype(BF16)
    rg_ref[...] = p[:, _C_RG:_C_KRA]
    kr_a = p[:, _C_KRA:_C_KRB]
    kr_b = p[:, _C_KRB:_C_END]

    tab = tab_ref[...]
    cos_q, sin_q = tab[:, 0:LANE], tab[:, LANE:2 * LANE]
    cos_k, sin_k = tab[:, 2 * LANE:3 * LANE], tab[:, 3 * LANE:4 * LANE]

    cqn = (_rms(p[:, _C_CQ:_C_CKV]) * qg_ref[...]).astype(BF16)
    q2 = _dot(cqn, wuq_ref[...])
    q = q2[:, :_HW] * _tile_heads(cos_q, MLA_HEADS) + q2[:, _HW:] * _tile_heads(sin_q, MLA_HEADS)
    q_ref[...] = (q * _Q_SCALE).astype(BF16)

    ckv = _rms(p[:, _C_CKV:_C_RQ]) * kvg_ref[...]
    ckv_b = ckv.astype(BF16)
    k_rope = kr_a * cos_k + kr_b * sin_k
    k_ref[...] = (_dot(ckv_b, wuk_ref[...]) + _tile_heads(k_rope, MLA_HEADS)).astype(BF16)
    vt_ref[...] = _value_rows_t(ckv_b, wvt_ref).astype(BF16)
    if with_cache:
        ckv_ref, kr_ref = refs[17:]
        ckv_ref[...] = ckv
        kr_ref[...] = kr_a


def _abproj(grp, layer, x, mods, g, table, win, qg, wuq, kvg, wuk, wvt, with_cache):
    t, d = x.shape
    row_outs = [(_HW, BF16), (_HW, BF16), None, (_RW, BF16), (_RW, BF16), (_RW, BF16), (_RW, F32)]
    if with_cache:
        row_outs += [(KV_LORA, F32), (LANE, F32)]
    out_specs = [grp.cols(_HW) if o is None else grp.rows(o[0]) for o in row_outs]
    out_shape = [jax.ShapeDtypeStruct((_HW, t), BF16) if o is None else jax.ShapeDtypeStruct((t, o[0]), o[1])
                 for o in row_outs]
    return pl.pallas_call(
        functools.partial(_abproj_kernel, with_cache=with_cache),
        grid=(grp.n_tiles,),
        in_specs=[grp.rows(d), grp.mods(layer, d), _resident((1, d)),
                  pl.BlockSpec((grp.tm, 4 * LANE), lambda i: (grp.table(i), 0)),
                  _resident(win.shape), _resident(qg.shape), _resident(wuq.shape),
                  _resident(kvg.shape), _resident(wuk.shape), _resident(wvt.shape)],
        out_specs=out_specs,
        out_shape=out_shape,
        compiler_params=_params(1),
        name=f"abproj_{grp.name}_{layer}",
    )(x, mods, g, table, win, qg, wuq, kvg, wuk, wvt)


def _ctxkv_kernel(ckv_ref, kr_ref, wuk_ref, wvt_ref, k_ref, vt_ref):
    ckv_b = ckv_ref[...].astype(BF16)
    k_ref[...] = (_dot(ckv_b, wuk_ref[...]) + _tile_heads(kr_ref[...], MLA_HEADS)).astype(BF16)
    vt_ref[...] = _value_rows_t(ckv_b, wvt_ref).astype(BF16)


def _ctxkv(layer_idx, cache_ckv, cache_kr_padded, wuk, wvt):
    b, _, past, kvl = cache_ckv.shape
    return pl.pallas_call(
        _ctxkv_kernel,
        grid=(b,),
        in_specs=[pl.BlockSpec((None, None, past, kvl), lambda i: (i, layer_idx, 0, 0)),
                  pl.BlockSpec((None, None, past, LANE), lambda i: (i, layer_idx, 0, 0)),
                  _resident(wuk.shape), _resident(wvt.shape)],
        out_specs=[pl.BlockSpec((past, _HW), lambda i: (i, 0)), pl.BlockSpec((_HW, past), lambda i: (0, i))],
        out_shape=[jax.ShapeDtypeStruct((b * past, _HW), BF16), jax.ShapeDtypeStruct((_HW, b * past), BF16)],
        compiler_params=_params(1),
        name=f"ctxkv_{layer_idx}",
    )(cache_ckv, cache_kr_padded, wuk, wvt)


def _attn_kernel(*refs, n_main, n_ctx):
    if n_ctx:
        q_ref, k_ref, vt_ref, kc_ref, vtc_ref, o_ref, s_scr = refs
    else:
        q_ref, k_ref, vt_ref, o_ref, s_scr = refs
        kc_ref = vtc_ref = None
    tq = q_ref.shape[0]
    kc = KEY_CHUNK
    chunks = [(kc_ref, vtc_ref, j) for j in range(n_ctx)] + [(k_ref, vt_ref, j) for j in range(n_main)]

    def score_chunk(h, c, mpart):
        kref, _, j = chunks[c]
        hs = slice(h * LANE, (h + 1) * LANE)
        s = _dot_nt(kref[j * kc:(j + 1) * kc, hs], q_ref[:, hs])
        s_scr[h % 2, c * kc:(c + 1) * kc, :] = s
        part = jnp.max(s.reshape(kc // 8, 8, tq), axis=0)
        return part if mpart is None else jnp.maximum(mpart, part)

    def value_chunk(h, c, m, acc):
        _, vref, j = chunks[c]
        p = jnp.exp2(s_scr[h % 2, c * kc:(c + 1) * kc, :] - m).astype(BF16)
        d = _dot(vref[h * LANE:(h + 1) * LANE, j * kc:(j + 1) * kc], p)
        return d if acc is None else acc + d

    m_prev = None
    for h in range(MLA_HEADS + 1):
        mpart, acc = None, None
        for c in range(len(chunks)):
            if h < MLA_HEADS:
                mpart = score_chunk(h, c, mpart)
            if h > 0:
                acc = value_chunk(h - 1, c, m_prev, acc)
        if h > 0:
            inv = 1.0 / acc[MLA_V:MLA_V + 1, :]
            o_ref[(h - 1) * MLA_V:h * MLA_V, :] = (acc[:MLA_V, :] * inv).astype(o_ref.dtype)
        if h < MLA_HEADS:
            m_prev = jnp.max(mpart, axis=0, keepdims=True)


def _attention(name, q, k, vt, n_seq, seq_len, ctx=None):
    tq = min(ATTN_Q_TILE, seq_len)
    qt = seq_len // tq
    ow = MLA_HEADS * MLA_V
    mode = dict(pipeline_mode=pl.Buffered(1)) if qt > 1 else {}
    in_specs = [pl.BlockSpec((tq, _HW), lambda b, i: (b * qt + i, 0)),
                pl.BlockSpec((seq_len, _HW), lambda b, i: (b, 0), **mode),
                pl.BlockSpec((_HW, seq_len), lambda b, i: (0, b), **mode)]
    args = [q, k, vt]
    past = 0
    if ctx is not None:
        kc, vtc, past = ctx
        in_specs += [pl.BlockSpec((past, _HW), lambda b, i: (b, 0), **mode),
                     pl.BlockSpec((_HW, past), lambda b, i: (0, b), **mode)]
        args += [kc, vtc]
    assert seq_len % KEY_CHUNK == 0 and past % KEY_CHUNK == 0
    return pl.pallas_call(
        functools.partial(_attn_kernel, n_main=seq_len // KEY_CHUNK, n_ctx=past // KEY_CHUNK),
        grid=(n_seq, qt),
        in_specs=in_specs,
        out_specs=pl.BlockSpec((ow, tq), lambda b, i: (0, b * qt + i)),
        out_shape=jax.ShapeDtypeStruct((ow, n_seq * seq_len), BF16),
        scratch_shapes=[pltpu.VMEM((2, seq_len + past, tq), F32)],
        compiler_params=_params(2),
        name=name,
    )(*args)


_RET_PAIR = 2
_RET_PW = _RET_PAIR * RET_DK


def _ret_kernel(*refs, n_chunks, has_s0, want_final):
    refs = list(refs)
    ld_ref, q_ref, k_ref, v_ref, g_ref = refs[:5]
    refs = refs[5:]
    s0_ref = refs.pop(0) if has_s0 else None
    o_ref = refs.pop(0)
    sfin_ref = refs.pop(0) if want_final else None
    cross_ref, sf_ref, sb_ref = refs

    ii = lax.broadcasted_iota(jnp.int32, (CHUNK, CHUNK), 0).astype(F32)
    jj = lax.broadcasted_iota(jnp.int32, (CHUNK, CHUNK), 1).astype(F32)
    rel = ii - jj
    consts = []
    for h in range(_RET_PAIR):
        lg_f = -jnp.exp(ld_ref[0, h])
        lg_b = -jnp.exp(ld_ref[1, h])
        mask = (jnp.where(rel >= 0, jnp.exp(lg_f * jnp.maximum(rel, 0.0)), 0.0)
                + jnp.where(rel <= 0, jnp.exp(lg_b * jnp.maximum(-rel, 0.0)), 0.0))
        consts.append(dict(
            mask=mask,
            qd_f=jnp.exp(lg_f * (ii + 1.0)), kd_f=jnp.exp(lg_f * (CHUNK - 1.0 - ii)), cd_f=jnp.exp(lg_f * CHUNK),
            qd_b=jnp.exp(lg_b * (CHUNK - ii)), kd_b=jnp.exp(lg_b * ii), cd_b=jnp.exp(lg_b * CHUNK)))
        if has_s0:
            sf_ref[h] = s0_ref[0, 0, h]
            sb_ref[h] = s0_ref[0, 1, h]
        else:
            sf_ref[h] = jnp.zeros((RET_DK, RET_DV), F32)
            sb_ref[h] = jnp.zeros((RET_DK, RET_DV), F32)

    def chunk_rows(n):
        return pl.ds(pl.multiple_of(n * CHUNK, CHUNK), CHUNK)

    def bwd_step(t, carry):
        rows = chunk_rows(n_chunks - 1 - t)
        for h in range(_RET_PAIR):
            c = consts[h]
            hs = slice(h * RET_DK, (h + 1) * RET_DK)
            q, k, v = q_ref[rows, hs], k_ref[rows, hs], v_ref[rows, hs]
            s = sb_ref[h]
            cross_ref[rows, hs] = _dot(q, s.astype(BF16)) * c["qd_b"]
            sb_ref[h] = s * c["cd_b"] + _dot_tn((k.astype(F32) * c["kd_b"]).astype(BF16), v)
        return carry

    def fwd_step(n, carry):
        rows = chunk_rows(n)
        for h in range(_RET_PAIR):
            c = consts[h]
            hs = slice(h * RET_DK, (h + 1) * RET_DK)
            q, k, v = q_ref[rows, hs], k_ref[rows, hs], v_ref[rows, hs]
            s = sf_ref[h]
            inner = _dot((_dot_nt(q, k) * c["mask"]).astype(BF16), v)
            o = inner + _dot(q, s.astype(BF16)) * c["qd_f"] + cross_ref[rows, hs]
            gate = g_ref[rows, hs]
            o_ref[rows, hs] = (_rms(o) * (gate * jax.nn.sigmoid(gate))).astype(o_ref.dtype)
            sf_ref[h] = s * c["cd_f"] + _dot_tn((k.astype(F32) * c["kd_f"]).astype(BF16), v)
        return carry

    lax.fori_loop(0, n_chunks, bwd_step, 0)
    lax.fori_loop(0, n_chunks, fwd_step, 0)
    if want_final:
        for h in range(_RET_PAIR):
            sfin_ref[0, 0, h] = sf_ref[h]
            sfin_ref[0, 1, h] = sb_ref[h]


def _retention(name, ld, rq, rk, rv, rg, n_seq, seq_len, s0=None, want_final=False):
    pairs = RET_HEADS // _RET_PAIR
    seq = pl.BlockSpec((seq_len, _RET_PW), lambda b, p: (b, p))
    state = pl.BlockSpec((1, 2, _RET_PAIR, RET_DK, RET_DV), lambda b, p: (b, 0, p, 0, 0))
    in_specs = [pl.BlockSpec((2, _RET_PAIR, CHUNK, CHUNK), lambda b, p: (0, p, 0, 0)), seq, seq, seq, seq]
    args = [ld, rq, rk, rv, rg]
    if s0 is not None:
        in_specs.append(state)
        args.append(s0)
    out_specs = [seq]
    out_shape = [jax.ShapeDtypeStruct((n_seq * seq_len, _RW), BF16)]
    if want_final:
        out_specs.append(state)
        out_shape.append(jax.ShapeDtypeStruct((n_seq, 2, RET_HEADS, RET_DK, RET_DV), F32))
    res = pl.pallas_call(
        functools.partial(_ret_kernel, n_chunks=seq_len // CHUNK, has_s0=s0 is not None, want_final=want_final),
        grid=(n_seq, pairs),
        in_specs=in_specs,
        out_specs=out_specs,
        out_shape=out_shape,
        scratch_shapes=[pltpu.VMEM((seq_len, _RET_PW), F32),
                        pltpu.VMEM((_RET_PAIR, RET_DK, RET_DV), F32),
                        pltpu.VMEM((_RET_PAIR, RET_DK, RET_DV), F32)],
        compiler_params=_params(2),
        name=name,
    )(*args)
    return res if want_final else res[0]


def _merge_ffn_kernel(x_ref, mod_ref, oat_ref, ret_ref, woa_ref, wor_ref, g_ref, wg_ref, wu_ref, wd_ref, o_ref):
    y = _dot_tn(oat_ref[...], woa_ref[...]) + _dot(ret_ref[...], wor_ref[...])
    x = x_ref[...] + mod_ref[0, 0, 2:3, :] * y
    o_ref[...] = _swiglu_residual(x, mod_ref, g_ref, wg_ref, wu_ref, wd_ref)


def _merge_ffn(grp, layer, x, mods, o_attn_t, ret, woa, wor, g, wg, wu, wd):
    t, d = x.shape
    f = wg.shape[1]
    return pl.pallas_call(
        _merge_ffn_kernel,
        grid=(grp.n_tiles,),
        in_specs=[grp.rows(d), grp.mods(layer, d), grp.cols(o_attn_t.shape[0]), grp.rows(ret.shape[1]),
                  _resident(woa.shape), _resident(wor.shape), _resident((1, d)),
                  _resident((d, f)), _resident((d, f)), _resident((f, d))],
        out_specs=grp.rows(d),
        out_shape=jax.ShapeDtypeStruct((t, d), F32),
        compiler_params=_params(1),
        name=f"merge_ffn_{grp.name}_{layer}",
    )(x, mods, o_attn_t, ret, woa, wor, g, wg, wu, wd)


def _rope_table(dec_seq, tm):
    rows = dec_seq // GRID_W
    row = jnp.repeat(jnp.arange(rows), GRID_W).astype(F32)
    col = jnp.tile(jnp.arange(GRID_W), rows).astype(F32)
    half = MLA_ROPE // 2
    inv = 1.0 / jnp.power(ROPE_BASE, jnp.arange(0, half, 2, dtype=F32) / half)
    ang = jnp.stack([row[:, None] * inv, col[:, None] * inv], axis=1)
    ang = jnp.stack([ang, ang], axis=2).reshape(dec_seq, MLA_ROPE)
    cos, sin = jnp.cos(ang), jnp.sin(ang)
    lo = MLA_NOPE + MLA_ROPE
    ones, zeros = jnp.ones((dec_seq, lo), F32), jnp.zeros((dec_seq, lo), F32)
    pos = jnp.concatenate([ones, cos, zeros, sin, zeros, cos, zeros, sin], axis=1)
    flat_k = jnp.concatenate([jnp.zeros((tm, MLA_NOPE), F32), jnp.ones((tm, MLA_ROPE), F32),
                              jnp.zeros((tm, MLA_ROPE), F32)], axis=1)
    flat = jnp.concatenate([jnp.ones((tm, LANE), F32), jnp.zeros((tm, LANE), F32), flat_k,
                            jnp.zeros((tm, LANE), F32)], axis=1)
    return jnp.concatenate([flat, pos], axis=0)


def _layout_w_in(w):
    d = w.shape[0]
    o = 0
    parts = {}
    for name, width in (("cq", Q_LORA), ("ckv", KV_LORA), ("kr", MLA_ROPE), ("rq", _RW), ("rk", _RW),
                        ("rv", _RW), ("rg", _RW)):
        parts[name] = w[:, o:o + width]
        o += width
    kr = parts["kr"]
    kr_a = jnp.concatenate([jnp.zeros((d, MLA_NOPE), w.dtype), kr, kr], axis=1)
    kr_b = jnp.concatenate([jnp.zeros((d, MLA_NOPE + MLA_ROPE), w.dtype), _rotate_half_axial(kr)], axis=1)
    return jnp.concatenate([parts["cq"], parts["ckv"], parts["rq"], parts["rk"], parts["rv"], parts["rg"],
                            kr_a, kr_b], axis=1).astype(BF16)


def _layout_w_uq(w):
    r = w.shape[0]
    wh = w.reshape(r, MLA_HEADS, MLA_NOPE + MLA_ROPE)
    nope, rope = wh[..., :MLA_NOPE], wh[..., MLA_NOPE:]
    main = jnp.concatenate([nope, rope, rope], axis=-1).reshape(r, _HW)
    rot = jnp.concatenate([jnp.zeros(nope.shape[:2] + (MLA_NOPE + MLA_ROPE,), w.dtype),
                           _rotate_half_axial(rope)], axis=-1).reshape(r, _HW)
    return jnp.concatenate([main, rot], axis=1).astype(BF16)


def _layout_w_ukv(w):
    r = w.shape[0]
    wh = w.reshape(r, MLA_HEADS, MLA_NOPE + MLA_V)
    kn, v = wh[..., :MLA_NOPE], wh[..., MLA_NOPE:]
    wuk = jnp.concatenate([kn, jnp.zeros((r, MLA_HEADS, LANE - MLA_NOPE), w.dtype)], axis=-1).reshape(r, _HW)
    wv = jnp.concatenate([v, jnp.zeros((r, MLA_HEADS, LANE - MLA_V), w.dtype)], axis=-1).reshape(r, _HW)
    return wuk.astype(BF16), wv.T.astype(BF16)


def kernel(x_prompt, x_sample, cache_mla_ckv, cache_mla_krope, state_ret, c, c_ctx, w_ada, b_ada, norm_g, w_in_ab, q_norm_g, w_uq, kv_norm_g, w_ukv, ret_log_decay, w_o_ab, w_in_c, ln_g_c, ln_b_c, w_s_c, b_s_c, w_out_c, w_ffn_gate, w_ffn_up, w_ffn_down, final_norm_g):
    batch, seq, d = x_prompt.shape
    dec_batch, dec_seq, _ = x_sample.shape
    depth = w_ada.shape[0]
    past = cache_mla_ckv.shape[2]
    groups = [_Group("p", batch, seq, 0, positional=False), _Group("s", dec_batch, dec_seq, 1, positional=True)]
    xs = [x_prompt.reshape(batch * seq, d), x_sample.reshape(dec_batch * dec_seq, d)]

    cond8 = jnp.zeros((8, d), F32).at[0].set(c_ctx).at[1:1 + dec_batch].set(c)
    mods = _ada_mods(cond8, w_ada, b_ada)

    table = _rope_table(dec_seq, ROW_TILE)
    cache_kr_padded = jnp.pad(cache_mla_krope, ((0, 0), (0, 0), (0, 0), (MLA_NOPE, LANE - MLA_NOPE - MLA_ROPE)))
    fg = final_norm_g.reshape(1, d)

    ckv_out, kr_out, state_out = [], [], []
    for l in range(depth):
        j = l // 2
        g1 = norm_g[l, 0].reshape(1, d)
        g2 = norm_g[l, 1].reshape(1, d)
        wg, wu, wd = w_ffn_gate[l].astype(BF16), w_ffn_up[l].astype(BF16), w_ffn_down[l].astype(BF16)
        last = l == depth - 1
        if l % 2 == 0:
            win, wuq = _layout_w_in(w_in_ab[j]), _layout_w_uq(w_uq[j])
            wuk, wvt = _layout_w_ukv(w_ukv[j])
            qg, kvg = q_norm_g[j].reshape(1, -1), kv_norm_g[j].reshape(1, -1)
            ld = jnp.broadcast_to(ret_log_decay[j][:, :, None, None], (2, RET_HEADS, CHUNK, CHUNK))
            wo = w_o_ab[j].astype(BF16)
            aw = MLA_HEADS * MLA_V
            kc, vtc = _ctxkv(j, cache_mla_ckv, cache_kr_padded, wuk, wvt)
            for gi, grp in enumerate(groups):
                prompt = gi == 0
                outs = _abproj(grp, l, xs[gi], mods, g1, table, win, qg, wuq, kvg, wuk, wvt, with_cache=prompt)
                q, k, vt, rq, rk, rv, rg = outs[:7]
                if prompt:
                    oat = _attention(f"attn_p_{l}", q, k, vt, grp.n_seq, grp.seq_len)
                    ret, s_fin = _retention(f"ret_p_{l}", ld, rq, rk, rv, rg, grp.n_seq, grp.seq_len,
                                            want_final=True)
                    ckv_out.append(outs[7].reshape(batch, seq, KV_LORA))
                    kr_out.append(outs[8][:, MLA_NOPE:MLA_NOPE + MLA_ROPE].reshape(batch, seq, MLA_ROPE))
                    state_out.append(s_fin)
                else:
                    oat = _attention(f"attn_s_{l}", q, k, vt, grp.n_seq, grp.seq_len, ctx=(kc, vtc, past))
                    ret = _retention(f"ret_s_{l}", ld, rq, rk, rv, rg, grp.n_seq, grp.seq_len,
                                     s0=state_ret[:, j])
                xs[gi] = _merge_ffn(grp, l, xs[gi], mods, oat, ret, wo[:aw], wo[aw:], g2, wg, wu, wd)
        else:
            gw = w_out_c.shape[1] // GM_GROUPS
            bs = jnp.broadcast_to(b_s_c[j][:, :, None], (GM_GROUPS, CHUNK, gw))
            win_c, ws, wout = w_in_c[j].astype(BF16), w_s_c[j].astype(BF16), w_out_c[j].astype(BF16)
            lng, lnb = ln_g_c[j].reshape(1, -1), ln_b_c[j].reshape(1, -1)
            for gi, grp in enumerate(groups):
                x1 = _cmix(grp, l, xs[gi], mods, g1, win_c, lng, lnb, ws, bs, wout)
                xs[gi] = _ffn(grp, l, x1, mods, g2, wg, wu, wd, fg, final_norm=last)

    y_prompt = xs[0].reshape(batch, seq, d)
    y_sample = xs[1].reshape(dec_batch, dec_seq, d)
    return (y_prompt, y_sample, jnp.stack(ckv_out, axis=1), jnp.stack(kr_out, axis=1),
            jnp.stack(state_out, axis=1))
```

```python
import functools
import math

import jax
import jax.numpy as jnp
from jax import lax
from jax.experimental import pallas as pl
from jax.experimental.pallas import tpu as pltpu

F32 = jnp.float32
BF16 = jnp.bfloat16
EPS = 1e-6

MLA_HEADS = 8
MLA_NOPE = 64
MLA_ROPE = 32
MLA_V = 64
Q_LORA = 256
KV_LORA = 128
ROPE_BASE = 10000.0
GRID_W = 64
RET_HEADS = 4
RET_DK = 128
RET_DV = 128
CHUNK = 128
GM_GROUPS = 8

LANE = 128
ROW_TILE = 512
ATTN_Q_TILE = 512
KEY_CHUNK = 256
SCORE_CHUNKS = 4
VMEM_LIMIT = 56 * 1024 * 1024


def _params(n_axes):
    return pltpu.CompilerParams(dimension_semantics=("arbitrary",) * n_axes,
                                vmem_limit_bytes=VMEM_LIMIT)


def _resident(shape):
    nd = len(shape)
    return pl.BlockSpec(shape, lambda *_: (0,) * nd, pipeline_mode=pl.Buffered(1))


def _rms(x):
    return x * lax.rsqrt(jnp.mean(x * x, axis=-1, keepdims=True) + EPS)


def _dot(a, b):
    return jnp.dot(a, b, preferred_element_type=F32)


def _dot_nt(a, b):
    return lax.dot_general(a, b, (((1,), (1,)), ((), ())), preferred_element_type=F32)


def _dot_tn(a, b):
    return lax.dot_general(a, b, (((0,), (0,)), ((), ())), preferred_element_type=F32)


def _tile_heads(x, n):
    return jnp.concatenate([x] * n, axis=1)


def _rotate_half_axial(x):
    xs = x.reshape(x.shape[:-1] + (2, 2, MLA_ROPE // 4))
    return jnp.stack([-xs[..., 1, :], xs[..., 0, :]], axis=-2).reshape(x.shape)


def _ada_kernel(cond_ref, w_ref, b_ref, o_ref):
    c = cond_ref[...]
    a = (c * jax.nn.sigmoid(c)).astype(BF16)
    o_ref[0] = _dot(a, w_ref[0].astype(BF16)) + b_ref[0]


def _ada_mods(cond8, w_ada, b_ada):
    depth, d, n = w_ada.shape
    tn = n // 4
    out = pl.pallas_call(
        _ada_kernel,
        grid=(depth, n // tn),
        in_specs=[pl.BlockSpec((8, d), lambda l, j: (0, 0)),
                  pl.BlockSpec((1, d, tn), lambda l, j: (l, 0, j)),
                  pl.BlockSpec((1, 1, tn), lambda l, j: (l, 0, j))],
        out_specs=pl.BlockSpec((1, 8, tn), lambda l, j: (l, 0, j)),
        out_shape=jax.ShapeDtypeStruct((depth, 8, n), F32),
        compiler_params=_params(2),
        name="ada_mod",
    )(cond8, w_ada, b_ada.reshape(depth, 1, n))
    return out.reshape(depth, 8, 6, d)


class _Group:
    def __init__(self, name, n_seq, seq_len, cond_base, positional):
        assert seq_len % ROW_TILE == 0 or ROW_TILE % seq_len == 0
        self.name = name
        self.n_seq, self.seq_len = n_seq, seq_len
        self.n_rows = n_seq * seq_len
        self.tm = ROW_TILE
        self.n_tiles = self.n_rows // ROW_TILE
        self.cond_base = cond_base
        self.tiles_per_cond = seq_len // ROW_TILE if positional else self.n_tiles
        self.positional = positional

    def cond(self, i):
        return self.cond_base + i // self.tiles_per_cond

    def table(self, i):
        return 1 + i % self.tiles_per_cond if self.positional else 0

    def rows(self, width):
        return pl.BlockSpec((self.tm, width), lambda i: (i, 0))

    def cols(self, height):
        return pl.BlockSpec((height, self.tm), lambda i: (0, i))

    def mods(self, layer, d):
        return pl.BlockSpec((1, 1, 6, d), lambda i: (layer, self.cond(i), 0, 0))


def _modulated(x, g, shift, scale):
    return (_rms(x) * g) * (1.0 + scale) + shift


def _swiglu_residual(x, mod_ref, g_ref, wg_ref, wu_ref, wd_ref):
    hb = _modulated(x, g_ref[...], mod_ref[0, 0, 3:4, :], mod_ref[0, 0, 4:5, :]).astype(BF16)
    gate = _dot(hb, wg_ref[...])
    up = _dot(hb, wu_ref[...])
    act = (gate * jax.nn.sigmoid(gate) * up).astype(BF16)
    return x + mod_ref[0, 0, 5:6, :] * _dot(act, wd_ref[...])


def _ffn_kernel(x_ref, mod_ref, g_ref, wg_ref, wu_ref, wd_ref, fg_ref, o_ref, *, final_norm):
    y = _swiglu_residual(x_ref[...], mod_ref, g_ref, wg_ref, wu_ref, wd_ref)
    if final_norm:
        y = _rms(y) * fg_ref[...]
    o_ref[...] = y


def _ffn(grp, layer, x, mods, g, wg, wu, wd, fg, final_norm):
    t, d = x.shape
    f = wg.shape[1]
    return pl.pallas_call(
        functools.partial(_ffn_kernel, final_norm=final_norm),
        grid=(grp.n_tiles,),
        in_specs=[grp.rows(d), grp.mods(layer, d), _resident((1, d)),
                  _resident((d, f)), _resident((d, f)), _resident((f, d)), _resident((1, d))],
        out_specs=grp.rows(d),
        out_shape=jax.ShapeDtypeStruct((t, d), F32),
        compiler_params=_params(1),
        name=f"ffn_{grp.name}_{layer}",
    )(x, mods, g, wg, wu, wd, fg)


def _cmix_kernel(x_ref, mod_ref, g_ref, win_ref, lng_ref, lnb_ref, ws_ref, bs_ref, wout_ref, o_ref, gated_ref):
    x = x_ref[...]
    tm = x.shape[0]
    w = lng_ref.shape[1]
    hb = _modulated(x, g_ref[...], mod_ref[0, 0, 0:1, :], mod_ref[0, 0, 1:2, :]).astype(BF16)
    uv = jax.nn.gelu(_dot(hb, win_ref[...]))
    u = uv[:, :w]
    v = uv[:, w:]
    mu = jnp.mean(v, axis=-1, keepdims=True)
    vc = v - mu
    var = jnp.mean(vc * vc, axis=-1, keepdims=True)
    vb = (vc * lax.rsqrt(var + EPS) * lng_ref[...] + lnb_ref[...]).astype(BF16)
    gw = w // GM_GROUPS
    for r in range(tm // CHUNK):
        rs = slice(r * CHUNK, (r + 1) * CHUNK)
        for gi in range(GM_GROUPS):
            cs = slice(gi * gw, (gi + 1) * gw)
            mixed = _dot(ws_ref[gi], vb[rs, cs]) + bs_ref[gi]
            gated_ref[rs, cs] = (u[rs, cs] * mixed).astype(BF16)
    o_ref[...] = x + mod_ref[0, 0, 2:3, :] * _dot(gated_ref[...], wout_ref[...])


def _cmix(grp, layer, x, mods, g, win, lng, lnb, ws, bs, wout):
    t, d = x.shape
    w = wout.shape[0]
    return pl.pallas_call(
        _cmix_kernel,
        grid=(grp.n_tiles,),
        in_specs=[grp.rows(d), grp.mods(layer, d), _resident((1, d)), _resident((d, 2 * w)),
                  _resident((1, w)), _resident((1, w)), _resident(ws.shape), _resident(bs.shape),
                  _resident((w, d))],
        out_specs=grp.rows(d),
        out_shape=jax.ShapeDtypeStruct((t, d), F32),
        scratch_shapes=[pltpu.VMEM((grp.tm, w), BF16)],
        compiler_params=_params(1),
        name=f"cmix_{grp.name}_{layer}",
    )(x, mods, g, win, lng, lnb, ws, bs, wout)


_HW = MLA_HEADS * LANE
_RW = RET_HEADS * RET_DK
_C_CQ = 0
_C_CKV = _C_CQ + Q_LORA
_C_RQ = _C_CKV + KV_LORA
_C_RK = _C_RQ + _RW
_C_RV = _C_RK + _RW
_C_RG = _C_RV + _RW
_C_KRA = _C_RG + _RW
_C_KRB = _C_KRA + LANE
_C_END = _C_KRB + LANE

_Q_SCALE = (MLA_NOPE + MLA_ROPE) ** -0.5 * math.log2(math.e)


def _value_rows_t(ckv_b, wvt_ref):
    vt = _dot_nt(wvt_ref[...], ckv_b)
    row = lax.broadcasted_iota(jnp.int32, vt.shape, 0)
    return jnp.where((row & (LANE - 1)) == MLA_V, 1.0, vt)


def _abproj_kernel(*refs, with_cache):
    (x_ref, mod_ref, g_ref, tab_ref, win_ref, qg_ref, wuq_ref, kvg_ref, wuk_ref, wvt_ref,
     q_ref, k_ref, vt_ref, rq_ref, rk_ref, rv_ref, rg_ref) = refs[:17]
    x = x_ref[...]
    hb = _modulated(x, g_ref[...], mod_ref[0, 0, 0:1, :], mod_ref[0, 0, 1:2, :]).astype(BF16)
    p = _dot(hb, win_ref[...])
    rq_ref[...] = p[:, _C_RQ:_C_RK].astype(BF16)
    rk_ref[...] = (p[:, _C_RK:_C_RV] * (RET_DK ** -0.5)).astype(BF16)
    rv_ref[...] = p[:, _C_RV:_C_RG].astype(BF16)
    rg_ref[...] = p[:, _C_RG:_C_KRA]
    kr_a = p[:, _C_KRA:_C_KRB]
    kr_b = p[:, _C_KRB:_C_END]

    tab = tab_ref[...]
    cos_q, sin_q = tab[:, 0:LANE], tab[:, LANE:2 * LANE]
    cos_k, sin_k = tab[:, 2 * LANE:3 * LANE], tab[:, 3 * LANE:4 * LANE]

    cqn = (_rms(p[:, _C_CQ:_C_CKV]) * qg_ref[...]).astype(BF16)
    q2 = _dot(cqn, wuq_ref[...])
    q = q2[:, :_HW] * _tile_heads(cos_q, MLA_HEADS) + q2[:, _HW:] * _tile_heads(sin_q, MLA_HEADS)
    q_ref[...] = (q * _Q_SCALE).astype(BF16)

    ckv = _rms(p[:, _C_CKV:_C_RQ]) * kvg_ref[...]
    ckv_b = ckv.astype(BF16)
    k_rope = kr_a * cos_k + kr_b * sin_k
    k_ref[...] = (_dot(ckv_b, wuk_ref[...]) + _tile_heads(k_rope, MLA_HEADS)).astype(BF16)
    vt_ref[...] = _value_rows_t(ckv_b, wvt_ref).astype(BF16)
    if with_cache:
        ckv_ref, kr_ref = refs[17:]
        ckv_ref[...] = ckv
        kr_ref[...] = kr_a


def _abproj(grp, layer, x, mods, g, table, win, qg, wuq, kvg, wuk, wvt, with_cache):
    t, d = x.shape
    row_outs = [(_HW, BF16), (_HW, BF16), None, (_RW, BF16), (_RW, BF16), (_RW, BF16), (_RW, F32)]
    if with_cache:
        row_outs += [(KV_LORA, F32), (LANE, F32)]
    out_specs = [grp.cols(_HW) if o is None else grp.rows(o[0]) for o in row_outs]
    out_shape = [jax.ShapeDtypeStruct((_HW, t), BF16) if o is None else jax.ShapeDtypeStruct((t, o[0]), o[1])
                 for o in row_outs]
    return pl.pallas_call(
        functools.partial(_abproj_kernel, with_cache=with_cache),
        grid=(grp.n_tiles,),
        in_specs=[grp.rows(d), grp.mods(layer, d), _resident((1, d)),
                  pl.BlockSpec((grp.tm, 4 * LANE), lambda i: (grp.table(i), 0)),
                  _resident(win.shape), _resident(qg.shape), _resident(wuq.shape),
                  _resident(kvg.shape), _resident(wuk.shape), _resident(wvt.shape)],
        out_specs=out_specs,
        out_shape=out_shape,
        compiler_params=_params(1),
        name=f"abproj_{grp.name}_{layer}",
    )(x, mods, g, table, win, qg, wuq, kvg, wuk, wvt)


def _ctxkv_kernel(ckv_ref, kr_ref, wuk_ref, wvt_ref, k_ref, vt_ref):
    ckv_b = ckv_ref[...].astype(BF16)
    k_ref[...] = (_dot(ckv_b, wuk_ref[...]) + _tile_heads(kr_ref[...], MLA_HEADS)).astype(BF16)
    vt_ref[...] = _value_rows_t(ckv_b, wvt_ref).astype(BF16)


def _ctxkv(layer_idx, cache_ckv, cache_kr_padded, wuk, wvt):
    b, _, past, kvl = cache_ckv.shape
    return pl.pallas_call(
        _ctxkv_kernel,
        grid=(b,),
        in_specs=[pl.BlockSpec((None, None, past, kvl), lambda i: (i, layer_idx, 0, 0)),
                  pl.BlockSpec((None, None, past, LANE), lambda i: (i, layer_idx, 0, 0)),
                  _resident(wuk.shape), _resident(wvt.shape)],
        out_specs=[pl.BlockSpec((past, _HW), lambda i: (i, 0)), pl.BlockSpec((_HW, past), lambda i: (0, i))],
        out_shape=[jax.ShapeDtypeStruct((b * past, _HW), BF16), jax.ShapeDtypeStruct((_HW, b * past), BF16)],
        compiler_params=_params(1),
        name=f"ctxkv_{layer_idx}",
    )(cache_ckv, cache_kr_padded, wuk, wvt)


def _attn_kernel(*refs, n_main, n_ctx):
    if n_ctx:
        q_ref, k_ref, vt_ref, kc_ref, vtc_ref, o_ref, s_scr, p_scr = refs
    else:
        q_ref, k_ref, vt_ref, o_ref, s_scr, p_scr = refs
        kc_ref = vtc_ref = None
    tq = q_ref.shape[0]
    kc = KEY_CHUNK
    chunks = [(kc_ref, vtc_ref, j) for j in range(n_ctx)] + [(k_ref, vt_ref, j) for j in range(n_main)]
    nc = len(chunks)
    score_starts = list(range(0, n_ctx, SCORE_CHUNKS)) + list(range(n_ctx, nc, SCORE_CHUNKS))

    def score_block(h, c0, mpart):
        kref, _, j = chunks[c0]
        n = min(SCORE_CHUNKS, (n_ctx if c0 < n_ctx else nc) - c0)
        hs = slice(h * LANE, (h + 1) * LANE)
        s = _dot_nt(kref[j * kc:(j + n) * kc, hs], q_ref[:, hs])
        s_scr[h % 2, c0 * kc:(c0 + n) * kc, :] = s
        part = jnp.max(s.reshape(n * kc // 8, 8, tq), axis=0)
        return part if mpart is None else jnp.maximum(mpart, part)

    def exp_chunk(h, c, m):
        rows = slice(c * kc, (c + 1) * kc)
        p_scr[h % 2, rows, :] = jnp.exp2(s_scr[h % 2, rows, :] - m).astype(BF16)

    def value_chunk(h, c, acc):
        _, vref, j = chunks[c]
        d = _dot(vref[h * LANE:(h + 1) * LANE, j * kc:(j + 1) * kc], p_scr[h % 2, c * kc:(c + 1) * kc, :])
        return d if acc is None else acc + d

    row_max = {}
    for h in range(MLA_HEADS + 2):
        mpart, acc = None, None
        for c in range(nc):
            if h < MLA_HEADS and c in score_starts:
                mpart = score_block(h, c, mpart)
            if 1 <= h <= MLA_HEADS:
                exp_chunk(h - 1, c, row_max[h - 1])
            if h >= 2:
                acc = value_chunk(h - 2, c, acc)
        if h >= 2:
            inv = 1.0 / acc[MLA_V:MLA_V + 1, :]
            o_ref[(h - 2) * MLA_V:(h - 1) * MLA_V, :] = (acc[:MLA_V, :] * inv).astype(o_ref.dtype)
        if h < MLA_HEADS:
            row_max[h] = jnp.max(mpart, axis=0, keepdims=True)


def _attention(name, q, k, vt, n_seq, seq_len, ctx=None):
    tq = min(ATTN_Q_TILE, seq_len)
    qt = seq_len // tq
    ow = MLA_HEADS * MLA_V
    mode = dict(pipeline_mode=pl.Buffered(1)) if qt > 1 else {}
    in_specs = [pl.BlockSpec((tq, _HW), lambda b, i: (b * qt + i, 0)),
                pl.BlockSpec((seq_len, _HW), lambda b, i: (b, 0), **mode),
                pl.BlockSpec((_HW, seq_len), lambda b, i: (0, b), **mode)]
    args = [q, k, vt]
    past = 0
    if ctx is not None:
        kc, vtc, past = ctx
        in_specs += [pl.BlockSpec((past, _HW), lambda b, i: (b, 0), **mode),
                     pl.BlockSpec((_HW, past), lambda b, i: (0, b), **mode)]
        args += [kc, vtc]
    assert seq_len % KEY_CHUNK == 0 and past % KEY_CHUNK == 0
    return pl.pallas_call(
        functools.partial(_attn_kernel, n_main=seq_len // KEY_CHUNK, n_ctx=past // KEY_CHUNK),
        grid=(n_seq, qt),
        in_specs=in_specs,
        out_specs=pl.BlockSpec((ow, tq), lambda b, i: (0, b * qt + i)),
        out_shape=jax.ShapeDtypeStruct((ow, n_seq * seq_len), BF16),
        scratch_shapes=[pltpu.VMEM((2, seq_len + past, tq), F32), pltpu.VMEM((2, seq_len + past, tq), BF16)],
        compiler_params=_params(2),
        name=name,
    )(*args)


_RET_PAIR = 2
_RET_PW = _RET_PAIR * RET_DK
RET_UNROLL = 8
RET_BODY_UNITS = 16


def _ret_kernel(*refs, n_local, n_chunks, unroll, has_s0, want_final):
    refs = list(refs)
    ld_ref, q_ref, k_ref, v_ref, g_ref = refs[:5]
    refs = refs[5:]
    s0_ref = refs.pop(0) if has_s0 else None
    o_ref = refs.pop(0)
    sfin_ref = refs.pop(0) if want_final else None
    cross_ref, sf_ref, sb_ref = refs
    seq_len = n_chunks * CHUNK

    ii = lax.broadcasted_iota(jnp.int32, (CHUNK, CHUNK), 0).astype(F32)
    jj = lax.broadcasted_iota(jnp.int32, (CHUNK, CHUNK), 1).astype(F32)
    rel = ii - jj
    consts = []
    for h in range(_RET_PAIR):
        lg_f = -jnp.exp(ld_ref[0, h])
        lg_b = -jnp.exp(ld_ref[1, h])
        mask = (jnp.where(rel >= 0, jnp.exp(lg_f * jnp.maximum(rel, 0.0)), 0.0)
                + jnp.where(rel <= 0, jnp.exp(lg_b * jnp.maximum(-rel, 0.0)), 0.0))
        consts.append(dict(
            mask=mask,
            qd_f=jnp.exp(lg_f * (ii + 1.0)), kd_f=jnp.exp(lg_f * (CHUNK - 1.0 - ii)), cd_f=jnp.exp(lg_f * CHUNK),
            qd_b=jnp.exp(lg_b * (CHUNK - ii)), kd_b=jnp.exp(lg_b * ii), cd_b=jnp.exp(lg_b * CHUNK)))
    for s in range(n_local):
        for h in range(_RET_PAIR):
            if has_s0:
                sf_ref[s, h] = s0_ref[s, 0, h]
                sb_ref[s, h] = s0_ref[s, 1, h]
            else:
                sf_ref[s, h] = jnp.zeros((RET_DK, RET_DV), F32)
                sb_ref[s, h] = jnp.zeros((RET_DK, RET_DV), F32)

    units = [(s, h, u) for s in range(n_local) for h in range(_RET_PAIR) for u in range(unroll)]

    def rows_of(s, n):
        start = s * seq_len + n * CHUNK
        if isinstance(start, int):
            return slice(start, start + CHUNK)
        return pl.ds(pl.multiple_of(start, CHUNK), CHUNK)

    def head_lanes(h):
        return slice(h * RET_DK, (h + 1) * RET_DK)

    def bwd_block(t):
        loaded, kv = {}, {}
        for (s, h, u) in units:
            rows, hs = rows_of(s, n_chunks - 1 - (t * unroll + u)), head_lanes(h)
            loaded[s, h, u] = (rows, hs, q_ref[rows, hs])
            kv[s, h, u] = _dot_tn((k_ref[rows, hs].astype(F32) * consts[h]["kd_b"]).astype(BF16), v_ref[rows, hs])
        for s in range(n_local):
            for h in range(_RET_PAIR):
                c = consts[h]
                state = sb_ref[s, h]
                for u in range(unroll):
                    rows, hs, q = loaded[s, h, u]
                    cross_ref[rows, hs] = _dot(q, state.astype(BF16)) * c["qd_b"]
                    state = state * c["cd_b"] + kv[s, h, u]
                sb_ref[s, h] = state

    def fwd_block(t):
        loaded = {}
        for (s, h, u) in units:
            rows, hs = rows_of(s, t * unroll + u), head_lanes(h)
            loaded[s, h, u] = (rows, hs, q_ref[rows, hs], k_ref[rows, hs], v_ref[rows, hs])
        qk, kv = {}, {}
        for key in units:
            rows, hs, q, k, v = loaded[key]
            qk[key] = _dot_nt(q, k)
            kv[key] = _dot_tn((k.astype(F32) * consts[key[1]]["kd_f"]).astype(BF16), v)
        inner, cross = {}, {}
        for s in range(n_local):
            for h in range(_RET_PAIR):
                c = consts[h]
                state = sf_ref[s, h]
                for u in range(unroll):
                    rows, hs, q, k, v = loaded[s, h, u]
                    inner[s, h, u] = _dot((qk[s, h, u] * c["mask"]).astype(BF16), v)
                    cross[s, h, u] = _dot(q, state.astype(BF16))
                    state = state * c["cd_f"] + kv[s, h, u]
                sf_ref[s, h] = state
        for key in units:
            rows, hs = loaded[key][:2]
            o = inner[key] + cross[key] * consts[key[1]]["qd_f"] + cross_ref[rows, hs]
            gate = g_ref[rows, hs]
            o_ref[rows, hs] = (_rms(o) * (gate * jax.nn.sigmoid(gate))).astype(o_ref.dtype)

    n_blocks = n_chunks // unroll
    if n_blocks == 1:
        bwd_block(0)
        fwd_block(0)
    else:
        def loop_body(block):
            def body(t, carry):
                block(t)
                return carry
            return body
        lax.fori_loop(0, n_blocks, loop_body(bwd_block), 0)
        lax.fori_loop(0, n_blocks, loop_body(fwd_block), 0)
    if want_final:
        for s in range(n_local):
            for h in range(_RET_PAIR):
                sfin_ref[s, 0, h] = sf_ref[s, h]
                sfin_ref[s, 1, h] = sb_ref[s, h]


def _retention(name, ld, rq, rk, rv, rg, n_seq, seq_len, s0=None, want_final=False):
    pairs = RET_HEADS // _RET_PAIR
    n_chunks = seq_len // CHUNK
    unroll = min(RET_UNROLL, n_chunks)
    n_local = max(1, RET_BODY_UNITS // n_chunks)
    assert n_chunks % unroll == 0 and n_seq % n_local == 0
    seq = pl.BlockSpec((n_local * seq_len, _RET_PW), lambda b, p: (b, p))
    state = pl.BlockSpec((n_local, 2, _RET_PAIR, RET_DK, RET_DV), lambda b, p: (b, 0, p, 0, 0))
    in_specs = [pl.BlockSpec((2, _RET_PAIR, CHUNK, CHUNK), lambda b, p: (0, p, 0, 0)), seq, seq, seq, seq]
    args = [ld, rq, rk, rv, rg]
    if s0 is not None:
        in_specs.append(state)
        args.append(s0)
    out_specs = [seq]
    out_shape = [jax.ShapeDtypeStruct((n_seq * seq_len, _RW), BF16)]
    if want_final:
        out_specs.append(state)
        out_shape.append(jax.ShapeDtypeStruct((n_seq, 2, RET_HEADS, RET_DK, RET_DV), F32))
    res = pl.pallas_call(
        functools.partial(_ret_kernel, n_local=n_local, n_chunks=n_chunks, unroll=unroll,
                          has_s0=s0 is not None, want_final=want_final),
        grid=(n_seq // n_local, pairs),
        in_specs=in_specs,
        out_specs=out_specs,
        out_shape=out_shape,
        scratch_shapes=[pltpu.VMEM((n_local * seq_len, _RET_PW), F32),
                        pltpu.VMEM((n_local, _RET_PAIR, RET_DK, RET_DV), F32),
                        pltpu.VMEM((n_local, _RET_PAIR, RET_DK, RET_DV), F32)],
        compiler_params=_params(2),
        name=name,
    )(*args)
    return res if want_final else res[0]


def _merge_ffn_kernel(x_ref, mod_ref, oat_ref, ret_ref, woa_ref, wor_ref, g_ref, wg_ref, wu_ref, wd_ref, o_ref):
    y = _dot_tn(oat_ref[...], woa_ref[...]) + _dot(ret_ref[...], wor_ref[...])
    x = x_ref[...] + mod_ref[0, 0, 2:3, :] * y
    o_ref[...] = _swiglu_residual(x, mod_ref, g_ref, wg_ref, wu_ref, wd_ref)


def _merge_ffn(grp, layer, x, mods, o_attn_t, ret, woa, wor, g, wg, wu, wd):
    t, d = x.shape
    f = wg.shape[1]
    return pl.pallas_call(
        _merge_ffn_kernel,
        grid=(grp.n_tiles,),
        in_specs=[grp.rows(d), grp.mods(layer, d), grp.cols(o_attn_t.shape[0]), grp.rows(ret.shape[1]),
                  _resident(woa.shape), _resident(wor.shape), _resident((1, d)),
                  _resident((d, f)), _resident((d, f)), _resident((f, d))],
        out_specs=grp.rows(d),
        out_shape=jax.ShapeDtypeStruct((t, d), F32),
        compiler_params=_params(1),
        name=f"merge_ffn_{grp.name}_{layer}",
    )(x, mods, o_attn_t, ret, woa, wor, g, wg, wu, wd)


def _rope_table(dec_seq, tm):
    rows = dec_seq // GRID_W
    row = jnp.repeat(jnp.arange(rows), GRID_W).astype(F32)
    col = jnp.tile(jnp.arange(GRID_W), rows).astype(F32)
    half = MLA_ROPE // 2
    inv = 1.0 / jnp.power(ROPE_BASE, jnp.arange(0, half, 2, dtype=F32) / half)
    ang = jnp.stack([row[:, None] * inv, col[:, None] * inv], axis=1)
    ang = jnp.stack([ang, ang], axis=2).reshape(dec_seq, MLA_ROPE)
    cos, sin = jnp.cos(ang), jnp.sin(ang)
    lo = MLA_NOPE + MLA_ROPE
    ones, zeros = jnp.ones((dec_seq, lo), F32), jnp.zeros((dec_seq, lo), F32)
    pos = jnp.concatenate([ones, cos, zeros, sin, zeros, cos, zeros, sin], axis=1)
    flat_k = jnp.concatenate([jnp.zeros((tm, MLA_NOPE), F32), jnp.ones((tm, MLA_ROPE), F32),
                              jnp.zeros((tm, MLA_ROPE), F32)], axis=1)
    flat = jnp.concatenate([jnp.ones((tm, LANE), F32), jnp.zeros((tm, LANE), F32), flat_k,
                            jnp.zeros((tm, LANE), F32)], axis=1)
    return jnp.concatenate([flat, pos], axis=0)


def _layout_w_in(w):
    d = w.shape[0]
    o = 0
    parts = {}
    for name, width in (("cq", Q_LORA), ("ckv", KV_LORA), ("kr", MLA_ROPE), ("rq", _RW), ("rk", _RW),
                        ("rv", _RW), ("rg", _RW)):
        parts[name] = w[:, o:o + width]
        o += width
    kr = parts["kr"]
    kr_a = jnp.concatenate([jnp.zeros((d, MLA_NOPE), w.dtype), kr, kr], axis=1)
    kr_b = jnp.concatenate([jnp.zeros((d, MLA_NOPE + MLA_ROPE), w.dtype), _rotate_half_axial(kr)], axis=1)
    return jnp.concatenate([parts["cq"], parts["ckv"], parts["rq"], parts["rk"], parts["rv"], parts["rg"],
                            kr_a, kr_b], axis=1).astype(BF16)


def _layout_w_uq(w):
    r = w.shape[0]
    wh = w.reshape(r, MLA_HEADS, MLA_NOPE + MLA_ROPE)
    nope, rope = wh[..., :MLA_NOPE], wh[..., MLA_NOPE:]
    main = jnp.concatenate([nope, rope, rope], axis=-1).reshape(r, _HW)
    rot = jnp.concatenate([jnp.zeros(nope.shape[:2] + (MLA_NOPE + MLA_ROPE,), w.dtype),
                           _rotate_half_axial(rope)], axis=-1).reshape(r, _HW)
    return jnp.concatenate([main, rot], axis=1).astype(BF16)


def _layout_w_ukv(w):
    r = w.shape[0]
    wh = w.reshape(r, MLA_HEADS, MLA_NOPE + MLA_V)
    kn, v = wh[..., :MLA_NOPE], wh[..., MLA_NOPE:]
    wuk = jnp.concatenate([kn, jnp.zeros((r, MLA_HEADS, LANE - MLA_NOPE), w.dtype)], axis=-1).reshape(r, _HW)
    wv = jnp.concatenate([v, jnp.zeros((r, MLA_HEADS, LANE - MLA_V), w.dtype)], axis=-1).reshape(r, _HW)
    return wuk.astype(BF16), wv.T.astype(BF16)


def kernel(x_prompt, x_sample, cache_mla_ckv, cache_mla_krope, state_ret, c, c_ctx, w_ada, b_ada, norm_g, w_in_ab, q_norm_g, w_uq, kv_norm_g, w_ukv, ret_log_decay, w_o_ab, w_in_c, ln_g_c, ln_b_c, w_s_c, b_s_c, w_out_c, w_ffn_gate, w_ffn_up, w_ffn_down, final_norm_g):
    batch, seq, d = x_prompt.shape
    dec_batch, dec_seq, _ = x_sample.shape
    depth = w_ada.shape[0]
    assert depth % 2 == 0
    past = cache_mla_ckv.shape[2]
    groups =[_Group("p", batch, seq, 0, positional=False), _Group("s", dec_batch, dec_seq, 1, positional=True)]
    xs = [x_prompt.reshape(batch * seq, d), x_sample.reshape(dec_batch * dec_seq, d)]

    cond8 = jnp.zeros((8, d), F32).at[0].set(c_ctx).at[1:1 + dec_batch].set(c)
    mods = _ada_mods(cond8, w_ada, b_ada)

    table = _rope_table(dec_seq, ROW_TILE)
    cache_kr_padded = jnp.pad(cache_mla_krope, ((0, 0), (0, 0), (0, 0), (MLA_NOPE, LANE - MLA_NOPE - MLA_ROPE)))
    fg = final_norm_g.reshape(1, d)

    ckv_out, kr_out, state_out = [], [], []
    for l in range(depth):
        j = l // 2
        g1 = norm_g[l, 0].reshape(1, d)
        g2 = norm_g[l, 1].reshape(1, d)
        wg, wu, wd = w_ffn_gate[l].astype(BF16), w_ffn_up[l].astype(BF16), w_ffn_down[l].astype(BF16)
        last = l == depth - 1
        if l % 2 == 0:
            win, wuq = _layout_w_in(w_in_ab[j]), _layout_w_uq(w_uq[j])
            wuk, wvt = _layout_w_ukv(w_ukv[j])
            qg, kvg = q_norm_g[j].reshape(1, -1), kv_norm_g[j].reshape(1, -1)
            ld = jnp.broadcast_to(ret_log_decay[j][:, :, None, None], (2, RET_HEADS, CHUNK, CHUNK))
            wo = w_o_ab[j].astype(BF16)
            aw = MLA_HEADS * MLA_V
            kc, vtc = _ctxkv(j, cache_mla_ckv, cache_kr_padded, wuk, wvt)
            for gi, grp in enumerate(groups):
                prompt = gi == 0
                outs = _abproj(grp, l, xs[gi], mods, g1, table, win, qg, wuq, kvg, wuk, wvt, with_cache=prompt)
                q, k, vt, rq, rk, rv, rg = outs[:7]
                if prompt:
                    oat = _attention(f"attn_p_{l}", q, k, vt, grp.n_seq, grp.seq_len)
                    ret, s_fin = _retention(f"ret_p_{l}", ld, rq, rk, rv, rg, grp.n_seq, grp.seq_len,
                                            want_final=True)
                    ckv_out.append(outs[7].reshape(batch, seq, KV_LORA))
                    kr_out.append(outs[8][:, MLA_NOPE:MLA_NOPE + MLA_ROPE].reshape(batch, seq, MLA_ROPE))
                    state_out.append(s_fin)
                else:
                    oat = _attention(f"attn_s_{l}", q, k, vt, grp.n_seq, grp.seq_len, ctx=(kc, vtc, past))
                    ret = _retention(f"ret_s_{l}", ld, rq, rk, rv, rg, grp.n_seq, grp.seq_len,
                                     s0=state_ret[:, j])
                xs[gi] = _merge_ffn(grp, l, xs[gi], mods, oat, ret, wo[:aw], wo[aw:], g2, wg, wu, wd)
        else:
            gw = w_out_c.shape[1] // GM_GROUPS
            bs = jnp.broadcast_to(b_s_c[j][:, :, None], (GM_GROUPS, CHUNK, gw))
            win_c, ws, wout = w_in_c[j].astype(BF16), w_s_c[j].astype(BF16), w_out_c[j].astype(BF16)
            lng, lnb = ln_g_c[j].reshape(1, -1), ln_b_c[j].reshape(1, -1)
            for gi, grp in enumerate(groups):
                x1 = _cmix(grp, l, xs[gi], mods, g1, win_c, lng, lnb, ws, bs, wout)
                xs[gi] = _ffn(grp, l, x1, mods, g2, wg, wu, wd, fg, final_norm=last)

    y_prompt = xs[0].reshape(batch, seq, d)
    y_sample = xs[1].reshape(dec_batch, dec_seq, d)
    return (y_prompt, y_sample, jnp.stack(ckv_out, axis=1), jnp.stack(kr_out, axis=1),
            jnp.stack(state_out, axis=1))
```

```python
import functools
import math

import jax
import jax.numpy as jnp
import numpy as np
from jax import lax
from jax.experimental import pallas as pl
from jax.experimental.pallas import tpu as pltpu

F32 = jnp.float32
BF16 = jnp.bfloat16
EPS = 1e-6

MLA_HEADS = 8
MLA_NOPE = 64
MLA_ROPE = 32
MLA_V = 64
Q_LORA = 256
KV_LORA = 128
ROPE_BASE = 10000.0
GRID_W = 64
RET_HEADS = 4
RET_DK = 128
RET_DV = 128
CHUNK = 128
GM_GROUPS = 8

LANE = 128
ROW_TILE = 512
ATTN_Q_TILE = 512
KEY_CHUNK = 256
SCORE_CHUNKS = 4
VMEM_LIMIT = 56 * 1024 * 1024


def _params(n_axes):
    return pltpu.CompilerParams(dimension_semantics=("arbitrary",) * n_axes,
                                vmem_limit_bytes=VMEM_LIMIT)


def _resident(shape):
    nd = len(shape)
    return pl.BlockSpec(shape, lambda *_: (0,) * nd, pipeline_mode=pl.Buffered(1))


def _slab(arr, *lead, rows=None):
    shape = arr.shape[len(lead):]
    first = 0
    if rows is not None:
        first, n = rows
        shape = (n,) + shape[1:]
    idx = tuple(lead) + (first,) + (0,) * (len(shape) - 1)
    return arr, pl.BlockSpec((None,) * len(lead) + shape, lambda *_: idx, pipeline_mode=pl.Buffered(1))


def _call(kernel_fn, operands, **kw):
    return pl.pallas_call(kernel_fn, in_specs=[s for _, s in operands], **kw)(*[a for a, _ in operands])


def _rms(x):
    return x * lax.rsqrt(jnp.mean(x * x, axis=-1, keepdims=True) + EPS)


def _dot(a, b):
    return jnp.dot(a, b, preferred_element_type=F32)


def _dot_nt(a, b):
    return lax.dot_general(a, b, (((1,), (1,)), ((), ())), preferred_element_type=F32)


def _dot_tn(a, b):
    return lax.dot_general(a, b, (((0,), (0,)), ((), ())), preferred_element_type=F32)


def _tile_heads(x, n):
    return jnp.concatenate([x] * n, axis=1)


def _rotate_half_axial(x):
    xs = x.reshape(x.shape[:-1] + (2, 2, MLA_ROPE // 4))
    return jnp.stack([-xs[..., 1, :], xs[..., 0, :]], axis=-2).reshape(x.shape)


def _ada_kernel(cond_ref, w_ref, b_ref, o_ref):
    c = cond_ref[...]
    a = (c * jax.nn.sigmoid(c)).astype(BF16)
    o_ref[0] = _dot(a, w_ref[0].astype(BF16)) + b_ref[0]


def _ada_mods(cond8, w_ada, b_ada):
    depth, d, n = w_ada.shape
    tn = n // 4
    out = pl.pallas_call(
        _ada_kernel,
        grid=(depth, n // tn),
        in_specs=[pl.BlockSpec((8, d), lambda l, j: (0, 0)),
                  pl.BlockSpec((1, d, tn), lambda l, j: (l, 0, j)),
                  pl.BlockSpec((1, 1, tn), lambda l, j: (l, 0, j))],
        out_specs=pl.BlockSpec((1, 8, tn), lambda l, j: (l, 0, j)),
        out_shape=jax.ShapeDtypeStruct((depth, 8, n), F32),
        compiler_params=_params(2),
        name="ada_mod",
    )(cond8, w_ada, b_ada.reshape(depth, 1, n))
    return out.reshape(depth, 8, 6, d)


class _Group:
    def __init__(self, name, n_seq, seq_len, cond_base, positional):
        assert seq_len % ROW_TILE == 0 or ROW_TILE % seq_len == 0
        self.name = name
        self.n_seq, self.seq_len = n_seq, seq_len
        self.n_rows = n_seq * seq_len
        self.tm = ROW_TILE
        self.n_tiles = self.n_rows // ROW_TILE
        self.cond_base = cond_base
        self.tiles_per_cond = seq_len // ROW_TILE if positional else self.n_tiles
        self.positional = positional

    def cond(self, i):
        return self.cond_base + i // self.tiles_per_cond

    def table(self, i):
        return 1 + i % self.tiles_per_cond if self.positional else 0

    def rows(self, width):
        return pl.BlockSpec((self.tm, width), lambda i: (i, 0))

    def cols(self, height):
        return pl.BlockSpec((height, self.tm), lambda i: (0, i))

    def mods(self, layer, d):
        return pl.BlockSpec((1, 1, 6, d), lambda i: (layer, self.cond(i), 0, 0))


def _modulated(x, g, shift, scale):
    return (_rms(x) * g) * (1.0 + scale) + shift


def _swiglu_residual(x, mod_ref, g_ref, wg_ref, wu_ref, wd_ref):
    hb = _modulated(x, g_ref[...], mod_ref[0, 0, 3:4, :], mod_ref[0, 0, 4:5, :]).astype(BF16)
    gate = _dot(hb, wg_ref[...])
    up = _dot(hb, wu_ref[...])
    act = (gate * jax.nn.sigmoid(gate) * up).astype(BF16)
    return x + mod_ref[0, 0, 5:6, :] * _dot(act, wd_ref[...])


def _ffn_kernel(x_ref, mod_ref, g_ref, wg_ref, wu_ref, wd_ref, fg_ref, o_ref, *, final_norm):
    y = _swiglu_residual(x_ref[...], mod_ref, g_ref, wg_ref, wu_ref, wd_ref)
    if final_norm:
        y = _rms(y) * fg_ref[...]
    o_ref[...] = y


def _ffn(grp, layer, x, mods, w, final_norm):
    t, d = x.shape
    return _call(
        functools.partial(_ffn_kernel, final_norm=final_norm),
        [(x, grp.rows(d)), (mods, grp.mods(layer, d)), _slab(w["norm_g"], layer, 1),
         _slab(w["ffn_gate"], layer), _slab(w["ffn_up"], layer), _slab(w["ffn_down"], layer), _slab(w["final_g"])],
        grid=(grp.n_tiles,),
        out_specs=grp.rows(d),
        out_shape=jax.ShapeDtypeStruct((t, d), F32),
        compiler_params=_params(1),
        name=f"ffn_{grp.name}_{layer}",
    )


CMIX_SUB = 256


def _cmix_kernel(x_ref, mod_ref, g_ref, win_ref, lng_ref, lnb_ref, ws_ref, bs_ref, wout_ref, o_ref, gated_ref):
    tm = x_ref.shape[0]
    w = lng_ref.shape[1]
    gw = w // GM_GROUPS

    def project(i):
        x = x_ref[i * CMIX_SUB:(i + 1) * CMIX_SUB, :]
        hb = _modulated(x, g_ref[...], mod_ref[0, 0, 0:1, :], mod_ref[0, 0, 1:2, :]).astype(BF16)
        return _dot(hb, win_ref[...])

    def gate(i, uv):
        uv = jax.nn.gelu(uv)
        u = uv[:, :w]
        v = uv[:, w:]
        mu = jnp.mean(v, axis=-1, keepdims=True)
        vc = v - mu
        var = jnp.mean(vc * vc, axis=-1, keepdims=True)
        vb = (vc * lax.rsqrt(var + EPS) * lng_ref[...] + lnb_ref[...]).astype(BF16)
        for r in range(CMIX_SUB // CHUNK):
            rs = slice(r * CHUNK, (r + 1) * CHUNK)
            out_rows = slice(i * CMIX_SUB + r * CHUNK, i * CMIX_SUB + (r + 1) * CHUNK)
            for gi in range(GM_GROUPS):
                cs = slice(gi * gw, (gi + 1) * gw)
                mixed = _dot(ws_ref[gi], vb[rs, cs]) + bs_ref[gi]
                gated_ref[out_rows, cs] = (u[rs, cs] * mixed).astype(BF16)

    def project_out(i):
        rows = slice(i * CMIX_SUB, (i + 1) * CMIX_SUB)
        o_ref[rows, :] = x_ref[rows, :] + mod_ref[0, 0, 2:3, :] * _dot(gated_ref[rows, :], wout_ref[...])

    n_sub = tm // CMIX_SUB
    uv = project(0)
    for i in range(n_sub):
        uv_next = project(i + 1) if i + 1 < n_sub else None
        gate(i, uv)
        project_out(i)
        uv = uv_next


def _cmix(grp, layer, j, x, mods, w):
    t, d = x.shape
    width = w["out_c"].shape[1]
    return _call(
        _cmix_kernel,
        [(x, grp.rows(d)), (mods, grp.mods(layer, d)), _slab(w["norm_g"], layer, 0), _slab(w["in_c"], j),
         _slab(w["ln_g_c"], j), _slab(w["ln_b_c"], j), _slab(w["s_c"], j), _slab(w["bs_c"], j), _slab(w["out_c"], j)],
        grid=(grp.n_tiles,),
        out_specs=grp.rows(d),
        out_shape=jax.ShapeDtypeStruct((t, d), F32),
        scratch_shapes=[pltpu.VMEM((grp.tm, width), BF16)],
        compiler_params=_params(1),
        name=f"cmix_{grp.name}_{layer}",
    )


_HW = MLA_HEADS * LANE
_RW = RET_HEADS * RET_DK
_C_CQ = 0
_C_CKV = _C_CQ + Q_LORA
_C_RQ = _C_CKV + KV_LORA
_C_RK = _C_RQ + _RW
_C_RV = _C_RK + _RW
_C_RG = _C_RV + _RW
_C_KRA = _C_RG + _RW
_C_KRB = _C_KRA + LANE
_C_END = _C_KRB + LANE

_Q_SCALE = (MLA_NOPE + MLA_ROPE) ** -0.5 * math.log2(math.e)


def _value_rows_t(ckv_b, wvt_ref):
    vt = _dot_nt(wvt_ref[...], ckv_b)
    row = lax.broadcasted_iota(jnp.int32, vt.shape, 0)
    return jnp.where((row & (LANE - 1)) == MLA_V, 1.0, vt)


def _abproj_kernel(*refs, with_cache):
    (x_ref, mod_ref, g_ref, tab_ref, win_ref, qg_ref, wuq_ref, kvg_ref, wuk_ref, wvt_ref,
     q_ref, k_ref, vt_ref, rq_ref, rk_ref, rv_ref, rg_ref) = refs[:17]
    x = x_ref[...]
    hb = _modulated(x, g_ref[...], mod_ref[0, 0, 0:1, :], mod_ref[0, 0, 1:2, :]).astype(BF16)
    p = _dot(hb, win_ref[...])
    rq_ref[...] = p[:, _C_RQ:_C_RK].astype(BF16)
    rk_ref[...] = (p[:, _C_RK:_C_RV] * (RET_DK ** -0.5)).astype(BF16)
    rv_ref[...] = p[:, _C_RV:_C_RG].astype(BF16)
    rg_ref[...] = p[:, _C_RG:_C_KRA]
    kr_a = p[:, _C_KRA:_C_KRB]
    kr_b = p[:, _C_KRB:_C_END]

    tab = tab_ref[...]
    cos_q, sin_q = tab[:, 0:LANE], tab[:, LANE:2 * LANE]
    cos_k, sin_k = tab[:, 2 * LANE:3 * LANE], tab[:, 3 * LANE:4 * LANE]

    cqn = (_rms(p[:, _C_CQ:_C_CKV]) * qg_ref[...]).astype(BF16)
    q2 = _dot(cqn, wuq_ref[...])
    q = q2[:, :_HW] * _tile_heads(cos_q, MLA_HEADS) + q2[:, _HW:] * _tile_heads(sin_q, MLA_HEADS)
    q_ref[...] = (q * _Q_SCALE).astype(BF16)

    ckv = _rms(p[:, _C_CKV:_C_RQ]) * kvg_ref[...]
    ckv_b = ckv.astype(BF16)
    k_rope = kr_a * cos_k + kr_b * sin_k
    k_ref[...] = (_dot(ckv_b, wuk_ref[...]) + _tile_heads(k_rope, MLA_HEADS)).astype(BF16)
    vt_ref[...] = _value_rows_t(ckv_b, wvt_ref).astype(BF16)
    if with_cache:
        ckv_ref, kr_ref = refs[17:]
        ckv_ref[...] = ckv
        kr_ref[...] = kr_a


def _abproj(grp, layer, j, x, mods, w, table, win, wuq, wuk, wvt, with_cache):
    t, d = x.shape
    row_outs = [(_HW, BF16), (_HW, BF16), None, (_RW, BF16), (_RW, BF16), (_RW, BF16), (_RW, F32)]
    if with_cache:
        row_outs += [(KV_LORA, F32), (LANE, F32)]
    out_specs = [grp.cols(_HW) if o is None else grp.rows(o[0]) for o in row_outs]
    out_shape = [jax.ShapeDtypeStruct((_HW, t), BF16) if o is None else jax.ShapeDtypeStruct((t, o[0]), o[1])
                 for o in row_outs]
    return _call(
        functools.partial(_abproj_kernel, with_cache=with_cache),
        [(x, grp.rows(d)), (mods, grp.mods(layer, d)), _slab(w["norm_g"], layer, 0),
         (table, pl.BlockSpec((grp.tm, 4 * LANE), lambda i: (grp.table(i), 0))),
         _slab(win), _slab(w["q_norm_g"], j), _slab(wuq), _slab(w["kv_norm_g"], j), _slab(wuk), _slab(wvt)],
        grid=(grp.n_tiles,),
        out_specs=out_specs,
        out_shape=out_shape,
        compiler_params=_params(1),
        name=f"abproj_{grp.name}_{layer}",
    )


def _ctxkv_kernel(ckv_ref, kr_ref, wuk_ref, wvt_ref, k_ref, vt_ref):
    ckv_b = ckv_ref[...].astype(BF16)
    k_ref[...] = (_dot(ckv_b, wuk_ref[...]) + _tile_heads(kr_ref[...], MLA_HEADS)).astype(BF16)
    vt_ref[...] = _value_rows_t(ckv_b, wvt_ref).astype(BF16)


def _ctxkv(layer_idx, cache_ckv, cache_kr_padded, wuk, wvt):
    b, _, past, kvl = cache_ckv.shape
    return pl.pallas_call(
        _ctxkv_kernel,
        grid=(b,),
        in_specs=[pl.BlockSpec((None, None, past, kvl), lambda i: (i, layer_idx, 0, 0)),
                  pl.BlockSpec((None, None, past, LANE), lambda i: (i, layer_idx, 0, 0)),
                  _resident(wuk.shape), _resident(wvt.shape)],
        out_specs=[pl.BlockSpec((past, _HW), lambda i: (i, 0)), pl.BlockSpec((_HW, past), lambda i: (0, i))],
        out_shape=[jax.ShapeDtypeStruct((b * past, _HW), BF16), jax.ShapeDtypeStruct((_HW, b * past), BF16)],
        compiler_params=_params(1),
        name=f"ctxkv_{layer_idx}",
    )(cache_ckv, cache_kr_padded, wuk, wvt)


def _attn_kernel(*refs, n_main, n_ctx):
    if n_ctx:
        q_ref, k_ref, vt_ref, kc_ref, vtc_ref, o_ref, s_scr, p_scr = refs
    else:
        q_ref, k_ref, vt_ref, o_ref, s_scr, p_scr = refs
        kc_ref = vtc_ref = None
    tq = q_ref.shape[0]
    kc = KEY_CHUNK
    chunks = [(kc_ref, vtc_ref, j) for j in range(n_ctx)] + [(k_ref, vt_ref, j) for j in range(n_main)]
    nc = len(chunks)
    score_starts = list(range(0, n_ctx, SCORE_CHUNKS)) + list(range(n_ctx, nc, SCORE_CHUNKS))

    def score_block(h, c0, mpart):
        kref, _, j = chunks[c0]
        n = min(SCORE_CHUNKS, (n_ctx if c0 < n_ctx else nc) - c0)
        hs = slice(h * LANE, (h + 1) * LANE)
        s = _dot_nt(kref[j * kc:(j + n) * kc, hs], q_ref[:, hs])
        s_scr[h % 2, c0 * kc:(c0 + n) * kc, :] = s
        part = jnp.max(s.reshape(n * kc // 8, 8, tq), axis=0)
        return part if mpart is None else jnp.maximum(mpart, part)

    def exp_chunk(h, c, m):
        rows = slice(c * kc, (c + 1) * kc)
        p_scr[h % 2, rows, :] = jnp.exp2(s_scr[h % 2, rows, :] - m).astype(BF16)

    def value_chunk(h, c, acc):
        _, vref, j = chunks[c]
        d = _dot(vref[h * LANE:(h + 1) * LANE, j * kc:(j + 1) * kc], p_scr[h % 2, c * kc:(c + 1) * kc, :])
        return d if acc is None else acc + d

    row_max = {}
    for h in range(MLA_HEADS + 2):
        mpart, acc = None, None
        for c in range(nc):
            if h < MLA_HEADS and c in score_starts:
                mpart = score_block(h, c, mpart)
            if 1 <= h <= MLA_HEADS:
                exp_chunk(h - 1, c, row_max[h - 1])
            if h >= 2:
                acc = value_chunk(h - 2, c, acc)
        if h >= 2:
            inv = 1.0 / acc[MLA_V:MLA_V + 1, :]
            o_ref[(h - 2) * MLA_V:(h - 1) * MLA_V, :] = (acc[:MLA_V, :] * inv).astype(o_ref.dtype)
        if h < MLA_HEADS:
            row_max[h] = jnp.max(mpart, axis=0, keepdims=True)


def _attention(name, q, k, vt, n_seq, seq_len, ctx=None):
    tq = min(ATTN_Q_TILE, seq_len)
    qt = seq_len // tq
    ow = MLA_HEADS * MLA_V
    mode = dict(pipeline_mode=pl.Buffered(1)) if qt > 1 else {}
    in_specs = [pl.BlockSpec((tq, _HW), lambda b, i: (b * qt + i, 0)),
                pl.BlockSpec((seq_len, _HW), lambda b, i: (b, 0), **mode),
                pl.BlockSpec((_HW, seq_len), lambda b, i: (0, b), **mode)]
    args = [q, k, vt]
    past = 0
    if ctx is not None:
        kc, vtc, past = ctx
        in_specs += [pl.BlockSpec((past, _HW), lambda b, i: (b, 0), **mode),
                     pl.BlockSpec((_HW, past), lambda b, i: (0, b), **mode)]
        args += [kc, vtc]
    assert seq_len % KEY_CHUNK == 0 and past % KEY_CHUNK == 0
    return pl.pallas_call(
        functools.partial(_attn_kernel, n_main=seq_len // KEY_CHUNK, n_ctx=past // KEY_CHUNK),
        grid=(n_seq, qt),
        in_specs=in_specs,
        out_specs=pl.BlockSpec((ow, tq), lambda b, i: (0, b * qt + i)),
        out_shape=jax.ShapeDtypeStruct((ow, n_seq * seq_len), BF16),
        scratch_shapes=[pltpu.VMEM((2, seq_len + past, tq), F32), pltpu.VMEM((2, seq_len + past, tq), BF16)],
        compiler_params=_params(2),
        name=name,
    )(*args)


_RET_PAIR = 2
_RET_PW = _RET_PAIR * RET_DK
RET_UNROLL = 8
RET_BODY_UNITS = 16


def _ret_kernel(*refs, n_local, n_chunks, unroll, has_s0, has_prev, want_final):
    refs = list(refs)
    ld_ref, q_ref, k_ref, v_ref, g_ref = refs[:5]
    refs = refs[5:]
    s0_ref = refs.pop(0) if has_s0 else None
    if has_prev:
        refs.pop(0)
    o_ref = refs.pop(0)
    sfin_ref = refs.pop(0) if want_final else None
    cross_ref, sf_ref, sb_ref = refs
    seq_len = n_chunks * CHUNK

    ii = lax.broadcasted_iota(jnp.int32, (CHUNK, CHUNK), 0).astype(F32)
    jj = lax.broadcasted_iota(jnp.int32, (CHUNK, CHUNK), 1).astype(F32)
    rel = ii - jj
    consts = []
    for h in range(_RET_PAIR):
        lg_f = -jnp.exp(ld_ref[0, h])
        lg_b = -jnp.exp(ld_ref[1, h])
        mask = (jnp.where(rel >= 0, jnp.exp(lg_f * jnp.maximum(rel, 0.0)), 0.0)
                + jnp.where(rel <= 0, jnp.exp(lg_b * jnp.maximum(-rel, 0.0)), 0.0))
        consts.append(dict(
            mask=mask,
            qd_f=jnp.exp(lg_f * (ii + 1.0)), kd_f=jnp.exp(lg_f * (CHUNK - 1.0 - ii)), cd_f=jnp.exp(lg_f * CHUNK),
            qd_b=jnp.exp(lg_b * (CHUNK - ii)), kd_b=jnp.exp(lg_b * ii), cd_b=jnp.exp(lg_b * CHUNK)))
    for s in range(n_local):
        for h in range(_RET_PAIR):
            if has_s0:
                sf_ref[s, h] = s0_ref[s, 0, h]
                sb_ref[s, h] = s0_ref[s, 1, h]
            else:
                sf_ref[s, h] = jnp.zeros((RET_DK, RET_DV), F32)
                sb_ref[s, h] = jnp.zeros((RET_DK, RET_DV), F32)

    units = [(s, h, u) for s in range(n_local) for h in range(_RET_PAIR) for u in range(unroll)]

    def rows_of(s, n):
        start = s * seq_len + n * CHUNK
        if isinstance(start, int):
            return slice(start, start + CHUNK)
        return pl.ds(pl.multiple_of(start, CHUNK), CHUNK)

    def head_lanes(h):
        return slice(h * RET_DK, (h + 1) * RET_DK)

    def bwd_block(t):
        loaded, kv = {}, {}
        for (s, h, u) in units:
            rows, hs = rows_of(s, n_chunks - 1 - (t * unroll + u)), head_lanes(h)
            loaded[s, h, u] = (rows, hs, q_ref[rows, hs])
            kv[s, h, u] = _dot_tn((k_ref[rows, hs].astype(F32) * consts[h]["kd_b"]).astype(BF16), v_ref[rows, hs])
        for s in range(n_local):
            for h in range(_RET_PAIR):
                c = consts[h]
                state = sb_ref[s, h]
                for u in range(unroll):
                    rows, hs, q = loaded[s, h, u]
                    cross_ref[rows, hs] = _dot(q, state.astype(BF16)) * c["qd_b"]
                    state = state * c["cd_b"] + kv[s, h, u]
                sb_ref[s, h] = state

    def fwd_block(t):
        loaded = {}
        for (s, h, u) in units:
            rows, hs = rows_of(s, t * unroll + u), head_lanes(h)
            loaded[s, h, u] = (rows, hs, q_ref[rows, hs], k_ref[rows, hs], v_ref[rows, hs])
        qk, kv = {}, {}
        for key in units:
            rows, hs, q, k, v = loaded[key]
            qk[key] = _dot_nt(q, k)
            kv[key] = _dot_tn((k.astype(F32) * consts[key[1]]["kd_f"]).astype(BF16), v)
        inner, cross = {}, {}
        for s in range(n_local):
            for h in range(_RET_PAIR):
                c = consts[h]
                state = sf_ref[s, h]
                for u in range(unroll):
                    rows, hs, q, k, v = loaded[s, h, u]
                    inner[s, h, u] = _dot((qk[s, h, u] * c["mask"]).astype(BF16), v)
                    cross[s, h, u] = _dot(q, state.astype(BF16))
                    state = state * c["cd_f"] + kv[s, h, u]
                sf_ref[s, h] = state
        for key in units:
            rows, hs = loaded[key][:2]
            o = inner[key] + cross[key] * consts[key[1]]["qd_f"] + cross_ref[rows, hs]
            gate = g_ref[rows, hs]
            o_ref[rows, hs] = (_rms(o) * (gate * jax.nn.sigmoid(gate))).astype(o_ref.dtype)

    n_blocks = n_chunks // unroll
    if n_blocks == 1:
        bwd_block(0)
        fwd_block(0)
    else:
        def loop_body(block):
            def body(t, carry):
                block(t)
                return carry
            return body
        lax.fori_loop(0, n_blocks, loop_body(bwd_block), 0)
        lax.fori_loop(0, n_blocks, loop_body(fwd_block), 0)
    if want_final:
        for s in range(n_local):
            for h in range(_RET_PAIR):
                sfin_ref[s, 0, h] = sf_ref[s, h]
                sfin_ref[s, 1, h] = sb_ref[s, h]


def _retention(name, j, ld, rq, rk, rv, rg, n_seq, seq_len, s0=None, want_final=False, finals=None):
    pairs = RET_HEADS // _RET_PAIR
    n_chunks = seq_len // CHUNK
    n_layers = ld.shape[0]
    unroll = min(RET_UNROLL, n_chunks)
    n_local = max(1, RET_BODY_UNITS // n_chunks)
    assert n_chunks % unroll == 0 and n_seq % n_local == 0
    seq = pl.BlockSpec((n_local * seq_len, _RET_PW), lambda b, p: (b, p))
    state = pl.BlockSpec((n_local, None, 2, _RET_PAIR, RET_DK, RET_DV), lambda b, p: (b, j, 0, p, 0, 0))
    operands = [(ld, pl.BlockSpec((None, 2, _RET_PAIR, CHUNK, CHUNK), lambda b, p: (j, 0, p, 0, 0))),
                (rq, seq), (rk, seq), (rv, seq), (rg, seq)]
    if s0 is not None:
        operands.append((s0, state))
    aliases = {}
    if finals is not None:
        aliases = {len(operands): 1}
        operands.append((finals, pl.BlockSpec(memory_space=pl.ANY)))
    out_specs = [seq]
    out_shape = [jax.ShapeDtypeStruct((n_seq * seq_len, _RW), BF16)]
    if want_final:
        out_specs.append(state)
        out_shape.append(jax.ShapeDtypeStruct((n_seq, n_layers, 2, RET_HEADS, RET_DK, RET_DV), F32))
    res = _call(
        functools.partial(_ret_kernel, n_local=n_local, n_chunks=n_chunks, unroll=unroll,
                          has_s0=s0 is not None, has_prev=finals is not None, want_final=want_final),
        operands,
        grid=(n_seq // n_local, pairs),
        out_specs=out_specs,
        out_shape=out_shape,
        input_output_aliases=aliases,
        scratch_shapes=[pltpu.VMEM((n_local * seq_len, _RET_PW), F32),
                        pltpu.VMEM((n_local, _RET_PAIR, RET_DK, RET_DV), F32),
                        pltpu.VMEM((n_local, _RET_PAIR, RET_DK, RET_DV), F32)],
        compiler_params=_params(2),
        name=name,
    )
    return res if want_final else res[0]


def _merge_ffn_kernel(x_ref, mod_ref, oat_ref, ret_ref, woa_ref, wor_ref, g_ref, wg_ref, wu_ref, wd_ref, o_ref):
    y = _dot_tn(oat_ref[...], woa_ref[...]) + _dot(ret_ref[...], wor_ref[...])
    x = x_ref[...] + mod_ref[0, 0, 2:3, :] * y
    o_ref[...] = _swiglu_residual(x, mod_ref, g_ref, wg_ref, wu_ref, wd_ref)


def _merge_ffn(grp, layer, j, x, mods, o_attn_t, ret, w):
    t, d = x.shape
    aw = o_attn_t.shape[0]
    assert aw == ret.shape[1]
    return _call(
        _merge_ffn_kernel,
        [(x, grp.rows(d)), (mods, grp.mods(layer, d)), (o_attn_t, grp.cols(aw)), (ret, grp.rows(aw)),
         _slab(w["o_ab"], j, rows=(0, aw)), _slab(w["o_ab"], j, rows=(1, aw)), _slab(w["norm_g"], layer, 1),
         _slab(w["ffn_gate"], layer), _slab(w["ffn_up"], layer), _slab(w["ffn_down"], layer)],
        grid=(grp.n_tiles,),
        out_specs=grp.rows(d),
        out_shape=jax.ShapeDtypeStruct((t, d), F32),
        compiler_params=_params(1),
        name=f"merge_ffn_{grp.name}_{layer}",
    )


def _rope_table(dec_seq, tm):
    f32 = np.float32
    rows = dec_seq // GRID_W
    row = np.repeat(np.arange(rows), GRID_W).astype(f32)
    col = np.tile(np.arange(GRID_W), rows).astype(f32)
    half = MLA_ROPE // 2
    inv = (1.0 / np.power(f32(ROPE_BASE), np.arange(0, half, 2, dtype=f32) / f32(half))).astype(f32)
    ang = np.stack([row[:, None] * inv, col[:, None] * inv], axis=1)
    ang = np.stack([ang, ang], axis=2).reshape(dec_seq, MLA_ROPE)
    cos, sin = np.cos(ang).astype(f32), np.sin(ang).astype(f32)
    lo = MLA_NOPE + MLA_ROPE
    ones, zeros = np.ones((dec_seq, lo), f32), np.zeros((dec_seq, lo), f32)
    pos = np.concatenate([ones, cos, zeros, sin, zeros, cos, zeros, sin], axis=1)
    flat_k = np.concatenate([np.zeros((tm, MLA_NOPE), f32), np.ones((tm, MLA_ROPE), f32),
                             np.zeros((tm, MLA_ROPE), f32)], axis=1)
    flat = np.concatenate([np.ones((tm, LANE), f32), np.zeros((tm, LANE), f32), flat_k,
                           np.zeros((tm, LANE), f32)], axis=1)
    return jnp.asarray(np.concatenate([flat, pos], axis=0))


def _layout_w_in(w):
    w = w.astype(BF16)
    d = w.shape[0]
    o = 0
    parts = {}
    for name, width in (("cq", Q_LORA), ("ckv", KV_LORA), ("kr", MLA_ROPE), ("rq", _RW), ("rk", _RW),
                        ("rv", _RW), ("rg", _RW)):
        parts[name] = w[:, o:o + width]
        o += width
    kr = parts["kr"]
    kr_a = jnp.concatenate([jnp.zeros((d, MLA_NOPE), w.dtype), kr, kr], axis=1)
    kr_b = jnp.concatenate([jnp.zeros((d, MLA_NOPE + MLA_ROPE), w.dtype), _rotate_half_axial(kr)], axis=1)
    return jnp.concatenate([parts["cq"], parts["ckv"], parts["rq"], parts["rk"], parts["rv"], parts["rg"],
                            kr_a, kr_b], axis=1).astype(BF16)


def _layout_w_uq(w):
    w = w.astype(BF16)
    r = w.shape[0]
    wh = w.reshape(r, MLA_HEADS, MLA_NOPE + MLA_ROPE)
    nope, rope = wh[..., :MLA_NOPE], wh[..., MLA_NOPE:]
    main = jnp.concatenate([nope, rope, rope], axis=-1).reshape(r, _HW)
    rot = jnp.concatenate([jnp.zeros(nope.shape[:2] + (MLA_NOPE + MLA_ROPE,), w.dtype),
                           _rotate_half_axial(rope)], axis=-1).reshape(r, _HW)
    return jnp.concatenate([main, rot], axis=1).astype(BF16)


def _layout_w_ukv(w):
    w = w.astype(BF16)
    r = w.shape[0]
    wh = w.reshape(r, MLA_HEADS, MLA_NOPE + MLA_V)
    kn, v = wh[..., :MLA_NOPE], wh[..., MLA_NOPE:]
    wuk = jnp.concatenate([kn, jnp.zeros((r, MLA_HEADS, LANE - MLA_NOPE), w.dtype)], axis=-1).reshape(r, _HW)
    wv = jnp.concatenate([v, jnp.zeros((r, MLA_HEADS, LANE - MLA_V), w.dtype)], axis=-1).reshape(r, _HW)
    return wuk.astype(BF16), wv.T.astype(BF16)


def kernel(x_prompt, x_sample, cache_mla_ckv, cache_mla_krope, state_ret, c, c_ctx, w_ada, b_ada, norm_g, w_in_ab, q_norm_g, w_uq, kv_norm_g, w_ukv, ret_log_decay, w_o_ab, w_in_c, ln_g_c, ln_b_c, w_s_c, b_s_c, w_out_c, w_ffn_gate, w_ffn_up, w_ffn_down, final_norm_g):
    batch, seq, d = x_prompt.shape
    dec_batch, dec_seq, _ = x_sample.shape
    depth = w_ada.shape[0]
    assert depth % 2 == 0
    past = cache_mla_ckv.shape[2]
    groups =[_Group("p", batch, seq, 0, positional=False), _Group("s", dec_batch, dec_seq, 1, positional=True)]
    xs = [x_prompt.reshape(batch * seq, d), x_sample.reshape(dec_batch * dec_seq, d)]

    cond8 = jnp.zeros((8, d), F32).at[0].set(c_ctx).at[1:1 + dec_batch].set(c)
    mods = _ada_mods(cond8, w_ada, b_ada)

    table = _rope_table(dec_seq, ROW_TILE)
    cache_kr_padded = jnp.pad(cache_mla_krope, ((0, 0), (0, 0), (0, 0), (MLA_NOPE, LANE - MLA_NOPE - MLA_ROPE)))
    n_ab = w_in_ab.shape[0]
    gw = w_out_c.shape[1] // GM_GROUPS
    w = dict(
        norm_g=norm_g.reshape(depth, 2, 1, d), final_g=final_norm_g.reshape(1, d),
        ffn_gate=w_ffn_gate.astype(BF16), ffn_up=w_ffn_up.astype(BF16), ffn_down=w_ffn_down.astype(BF16),
        q_norm_g=q_norm_g.reshape(n_ab, 1, -1), kv_norm_g=kv_norm_g.reshape(n_ab, 1, -1), o_ab=w_o_ab.astype(BF16),
        in_c=w_in_c.astype(BF16), out_c=w_out_c.astype(BF16), s_c=w_s_c.astype(BF16),
        ln_g_c=ln_g_c[:, None, :], ln_b_c=ln_b_c[:, None, :],
        bs_c=jnp.broadcast_to(b_s_c[:, :, :, None], b_s_c.shape + (gw,)))
    ld = jnp.broadcast_to(ret_log_decay[:, :, :, None, None], ret_log_decay.shape + (CHUNK, CHUNK))

    ckv_out, kr_out, finals = [], [], None
    for l in range(depth):
        j = l // 2
        last = l == depth - 1
        if l % 2 == 0:
            win, wuq = _layout_w_in(w_in_ab[j]), _layout_w_uq(w_uq[j])
            wuk, wvt = _layout_w_ukv(w_ukv[j])
            kc, vtc = _ctxkv(j, cache_mla_ckv, cache_kr_padded, wuk, wvt)
            for gi, grp in enumerate(groups):
                prompt = gi == 0
                outs = _abproj(grp, l, j, xs[gi], mods, w, table, win, wuq, wuk, wvt, with_cache=prompt)
                q, k, vt, rq, rk, rv, rg = outs[:7]
                if prompt:
                    oat = _attention(f"attn_p_{l}", q, k, vt, grp.n_seq, grp.seq_len)
                    ret, finals = _retention(f"ret_p_{l}", j, ld, rq, rk, rv, rg, grp.n_seq, grp.seq_len,
                                             want_final=True, finals=finals)
                    ckv_out.append(outs[7].reshape(batch, seq, KV_LORA))
                    kr_out.append(outs[8][:, MLA_NOPE:MLA_NOPE + MLA_ROPE].reshape(batch, seq, MLA_ROPE))
                else:
                    oat = _attention(f"attn_s_{l}", q, k, vt, grp.n_seq, grp.seq_len, ctx=(kc, vtc, past))
                    ret = _retention(f"ret_s_{l}", j, ld, rq, rk, rv, rg, grp.n_seq, grp.seq_len, s0=state_ret)
                xs[gi] = _merge_ffn(grp, l, j, xs[gi], mods, oat, ret, w)
        else:
            for gi, grp in enumerate(groups):
                x1 = _cmix(grp, l, j, xs[gi], mods, w)
                xs[gi] = _ffn(grp, l, x1, mods, w, final_norm=last)

    y_prompt = xs[0].reshape(batch, seq, d)
    y_sample = xs[1].reshape(dec_batch, dec_seq, d)
    return (y_prompt, y_sample, jnp.stack(ckv_out, axis=1), jnp.stack(kr_out, axis=1), finals)
```

```python
import functools
import math

import jax
import jax.numpy as jnp
import numpy as np
from jax import lax
from jax.experimental import pallas as pl
from jax.experimental.pallas import tpu as pltpu

F32 = jnp.float32
BF16 = jnp.bfloat16
EPS = 1e-6

MLA_HEADS = 8
MLA_NOPE = 64
MLA_ROPE = 32
MLA_V = 64
Q_LORA = 256
KV_LORA = 128
ROPE_BASE = 10000.0
GRID_W = 64
RET_HEADS = 4
RET_DK = 128
RET_DV = 128
CHUNK = 128
GM_GROUPS = 8

LANE = 128
ROW_TILE = 512
ATTN_Q_TILE = 512
KEY_CHUNK = 256
SCORE_CHUNKS = 4
VMEM_LIMIT = 56 * 1024 * 1024


def _params(n_axes):
    return pltpu.CompilerParams(dimension_semantics=("arbitrary",) * n_axes,
                                vmem_limit_bytes=VMEM_LIMIT)


def _resident(shape):
    nd = len(shape)
    return pl.BlockSpec(shape, lambda *_: (0,) * nd, pipeline_mode=pl.Buffered(1))


def _slab(arr, *lead, rows=None):
    shape = arr.shape[len(lead):]
    first = 0
    if rows is not None:
        first, n = rows
        shape = (n,) + shape[1:]
    idx = tuple(lead) + (first,) + (0,) * (len(shape) - 1)
    return arr, pl.BlockSpec((None,) * len(lead) + shape, lambda *_: idx, pipeline_mode=pl.Buffered(1))


def _call(kernel_fn, operands, **kw):
    return pl.pallas_call(kernel_fn, in_specs=[s for _, s in operands], **kw)(*[a for a, _ in operands])


def _rms(x):
    return x * lax.rsqrt(jnp.mean(x * x, axis=-1, keepdims=True) + EPS)


def _dot(a, b):
    return jnp.dot(a, b, preferred_element_type=F32)


def _dot_nt(a, b):
    return lax.dot_general(a, b, (((1,), (1,)), ((), ())), preferred_element_type=F32)


def _dot_tn(a, b):
    return lax.dot_general(a, b, (((0,), (0,)), ((), ())), preferred_element_type=F32)


def _tile_heads(x, n):
    return jnp.concatenate([x] * n, axis=1)


def _rotate_half_axial(x):
    xs = x.reshape(x.shape[:-1] + (2, 2, MLA_ROPE // 4))
    return jnp.stack([-xs[..., 1, :], xs[..., 0, :]], axis=-2).reshape(x.shape)


def _ada_kernel(cond_ref, w_ref, b_ref, o_ref):
    c = cond_ref[...]
    a = (c * jax.nn.sigmoid(c)).astype(BF16)
    o_ref[0] = _dot(a, w_ref[0].astype(BF16)) + b_ref[0]


def _ada_mods(cond8, w_ada, b_ada):
    depth, d, n = w_ada.shape
    tn = n // 4
    out = pl.pallas_call(
        _ada_kernel,
        grid=(depth, n // tn),
        in_specs=[pl.BlockSpec((8, d), lambda l, j: (0, 0)),
                  pl.BlockSpec((1, d, tn), lambda l, j: (l, 0, j)),
                  pl.BlockSpec((1, 1, tn), lambda l, j: (l, 0, j))],
        out_specs=pl.BlockSpec((1, 8, tn), lambda l, j: (l, 0, j)),
        out_shape=jax.ShapeDtypeStruct((depth, 8, n), F32),
        compiler_params=_params(2),
        name="ada_mod",
    )(cond8, w_ada, b_ada.reshape(depth, 1, n))
    return out.reshape(depth, 8, 6, d)


class _Group:
    def __init__(self, name, n_seq, seq_len, cond_base, positional, tm=ROW_TILE):
        assert seq_len % tm == 0 or tm % seq_len == 0
        self.name = name
        self.n_seq, self.seq_len = n_seq, seq_len
        self.n_rows = n_seq * seq_len
        self.tm = tm
        self.n_tiles = self.n_rows // tm
        self.cond_base = cond_base
        self.tiles_per_cond = seq_len // tm if positional else self.n_tiles
        self.positional = positional

    def with_tile(self, tm):
        return _Group(self.name, self.n_seq, self.seq_len, self.cond_base, self.positional, tm)

    def cond(self, i):
        return self.cond_base + i // self.tiles_per_cond

    def table(self, i):
        return 1 + i % self.tiles_per_cond if self.positional else 0

    def rows(self, width):
        return pl.BlockSpec((self.tm, width), lambda i: (i, 0))

    def cols(self, height):
        return pl.BlockSpec((height, self.tm), lambda i: (0, i))

    def mods(self, layer, d):
        return pl.BlockSpec((1, 1, 6, d), lambda i: (layer, self.cond(i), 0, 0))


def _modulated(x, g, shift, scale):
    return (_rms(x) * g) * (1.0 + scale) + shift


FFN_ROW_TILE = 1024
FFN_SUB = 256


def _swiglu_pipeline(x_of, n_sub, mod_ref, g_ref, wg_ref, wu_ref, wd_ref, emit):
    def project(i):
        x = x_of(i)
        hb = _modulated(x, g_ref[...], mod_ref[0, 0, 3:4, :], mod_ref[0, 0, 4:5, :]).astype(BF16)
        return x, _dot(hb, wg_ref[...]), _dot(hb, wu_ref[...])

    def finish(i, x, gate, up):
        act = (gate * jax.nn.sigmoid(gate) * up).astype(BF16)
        emit(i, x + mod_ref[0, 0, 5:6, :] * _dot(act, wd_ref[...]))

    cur = project(0)
    for i in range(n_sub):
        nxt = project(i + 1) if i + 1 < n_sub else None
        finish(i, *cur)
        cur = nxt


def _sub_rows(i):
    return slice(i * FFN_SUB, (i + 1) * FFN_SUB)


def _ffn_kernel(x_ref, mod_ref, g_ref, wg_ref, wu_ref, wd_ref, fg_ref, o_ref, *, final_norm):
    def emit(i, y):
        if final_norm:
            y = _rms(y) * fg_ref[...]
        o_ref[_sub_rows(i), :] = y

    _swiglu_pipeline(lambda i: x_ref[_sub_rows(i), :], x_ref.shape[0] // FFN_SUB,
                     mod_ref, g_ref, wg_ref, wu_ref, wd_ref, emit)


def _ffn(grp, layer, x, mods, w, final_norm):
    t, d = x.shape
    return _call(
        functools.partial(_ffn_kernel, final_norm=final_norm),
        [(x, grp.rows(d)), (mods, grp.mods(layer, d)), _slab(w["norm_g"], layer, 1),
         _slab(w["ffn_gate"], layer), _slab(w["ffn_up"], layer), _slab(w["ffn_down"], layer), _slab(w["final_g"])],
        grid=(grp.n_tiles,),
        out_specs=grp.rows(d),
        out_shape=jax.ShapeDtypeStruct((t, d), F32),
        compiler_params=_params(1),
        name=f"ffn_{grp.name}_{layer}",
    )


CMIX_SUB = 256


def _cmix_kernel(x_ref, mod_ref, g_ref, win_ref, lng_ref, lnb_ref, ws_ref, bs_ref, wout_ref, o_ref, gated_ref):
    tm = x_ref.shape[0]
    w = lng_ref.shape[1]
    gw = w // GM_GROUPS

    def project(i):
        x = x_ref[i * CMIX_SUB:(i + 1) * CMIX_SUB, :]
        hb = _modulated(x, g_ref[...], mod_ref[0, 0, 0:1, :], mod_ref[0, 0, 1:2, :]).astype(BF16)
        return _dot(hb, win_ref[...])

    def gate(i, uv):
        uv = jax.nn.gelu(uv)
        u = uv[:, :w]
        v = uv[:, w:]
        mu = jnp.mean(v, axis=-1, keepdims=True)
        vc = v - mu
        var = jnp.mean(vc * vc, axis=-1, keepdims=True)
        vb = (vc * lax.rsqrt(var + EPS) * lng_ref[...] + lnb_ref[...]).astype(BF16)
        for r in range(CMIX_SUB // CHUNK):
            rs = slice(r * CHUNK, (r + 1) * CHUNK)
            out_rows = slice(i * CMIX_SUB + r * CHUNK, i * CMIX_SUB + (r + 1) * CHUNK)
            for gi in range(GM_GROUPS):
                cs = slice(gi * gw, (gi + 1) * gw)
                mixed = _dot(ws_ref[gi], vb[rs, cs]) + bs_ref[gi]
                gated_ref[out_rows, cs] = (u[rs, cs] * mixed).astype(BF16)

    def project_out(i):
        rows = slice(i * CMIX_SUB, (i + 1) * CMIX_SUB)
        o_ref[rows, :] = x_ref[rows, :] + mod_ref[0, 0, 2:3, :] * _dot(gated_ref[rows, :], wout_ref[...])

    n_sub = tm // CMIX_SUB
    uv = project(0)
    for i in range(n_sub):
        uv_next = project(i + 1) if i + 1 < n_sub else None
        gate(i, uv)
        project_out(i)
        uv = uv_next


def _cmix(grp, layer, j, x, mods, w):
    t, d = x.shape
    width = w["out_c"].shape[1]
    return _call(
        _cmix_kernel,
        [(x, grp.rows(d)), (mods, grp.mods(layer, d)), _slab(w["norm_g"], layer, 0), _slab(w["in_c"], j),
         _slab(w["ln_g_c"], j), _slab(w["ln_b_c"], j), _slab(w["s_c"], j), _slab(w["bs_c"], j), _slab(w["out_c"], j)],
        grid=(grp.n_tiles,),
        out_specs=grp.rows(d),
        out_shape=jax.ShapeDtypeStruct((t, d), F32),
        scratch_shapes=[pltpu.VMEM((grp.tm, width), BF16)],
        compiler_params=_params(1),
        name=f"cmix_{grp.name}_{layer}",
    )


_HW = MLA_HEADS * LANE
_RW = RET_HEADS * RET_DK
_C_CQ = 0
_C_CKV = _C_CQ + Q_LORA
_C_RQ = _C_CKV + KV_LORA
_C_RK = _C_RQ + _RW
_C_RV = _C_RK + _RW
_C_RG = _C_RV + _RW
_C_KRA = _C_RG + _RW
_C_KRB = _C_KRA + LANE
_C_END = _C_KRB + LANE

_Q_SCALE = (MLA_NOPE + MLA_ROPE) ** -0.5 * math.log2(math.e)


def _value_rows_t(ckv_b, wvt_ref):
    vt = _dot_nt(wvt_ref[...], ckv_b)
    row = lax.broadcasted_iota(jnp.int32, vt.shape, 0)
    return jnp.where((row & (LANE - 1)) == MLA_V, 1.0, vt)


def _abproj_kernel(*refs, with_cache):
    (x_ref, mod_ref, g_ref, tab_ref, win_ref, qg_ref, wuq_ref, kvg_ref, wuk_ref, wvt_ref,
     q_ref, k_ref, vt_ref, rq_ref, rk_ref, rv_ref, rg_ref) = refs[:17]
    x = x_ref[...]
    hb = _modulated(x, g_ref[...], mod_ref[0, 0, 0:1, :], mod_ref[0, 0, 1:2, :]).astype(BF16)
    p = _dot(hb, win_ref[...])
    rq_ref[...] = p[:, _C_RQ:_C_RK].astype(BF16)
    rk_ref[...] = (p[:, _C_RK:_C_RV] * (RET_DK ** -0.5)).astype(BF16)
    rv_ref[...] = p[:, _C_RV:_C_RG].astype(BF16)
    rg_ref[...] = p[:, _C_RG:_C_KRA]
    kr_a = p[:, _C_KRA:_C_KRB]
    kr_b = p[:, _C_KRB:_C_END]

    tab = tab_ref[...]
    cos_q, sin_q = tab[:, 0:LANE], tab[:, LANE:2 * LANE]
    cos_k, sin_k = tab[:, 2 * LANE:3 * LANE], tab[:, 3 * LANE:4 * LANE]

    cqn = (_rms(p[:, _C_CQ:_C_CKV]) * qg_ref[...]).astype(BF16)
    q2 = _dot(cqn, wuq_ref[...])
    q = q2[:, :_HW] * _tile_heads(cos_q, MLA_HEADS) + q2[:, _HW:] * _tile_heads(sin_q, MLA_HEADS)
    q_ref[...] = (q * _Q_SCALE).astype(BF16)

    ckv = _rms(p[:, _C_CKV:_C_RQ]) * kvg_ref[...]
    ckv_b = ckv.astype(BF16)
    k_rope = kr_a * cos_k + kr_b * sin_k
    k_ref[...] = (_dot(ckv_b, wuk_ref[...]) + _tile_heads(k_rope, MLA_HEADS)).astype(BF16)
    vt_ref[...] = _value_rows_t(ckv_b, wvt_ref).astype(BF16)
    if with_cache:
        ckv_ref, kr_ref = refs[17:]
        ckv_ref[...] = ckv
        kr_ref[...] = kr_a


def _abproj(grp, layer, j, x, mods, w, table, win, wuq, wuk, wvt, with_cache):
    t, d = x.shape
    row_outs = [(_HW, BF16), (_HW, BF16), None, (_RW, BF16), (_RW, BF16), (_RW, BF16), (_RW, F32)]
    if with_cache:
        row_outs += [(KV_LORA, F32), (LANE, F32)]
    out_specs = [grp.cols(_HW) if o is None else grp.rows(o[0]) for o in row_outs]
    out_shape = [jax.ShapeDtypeStruct((_HW, t), BF16) if o is None else jax.ShapeDtypeStruct((t, o[0]), o[1])
                 for o in row_outs]
    return _call(
        functools.partial(_abproj_kernel, with_cache=with_cache),
        [(x, grp.rows(d)), (mods, grp.mods(layer, d)), _slab(w["norm_g"], layer, 0),
         (table, pl.BlockSpec((grp.tm, 4 * LANE), lambda i: (grp.table(i), 0))),
         _slab(win), _slab(w["q_norm_g"], j), _slab(wuq), _slab(w["kv_norm_g"], j), _slab(wuk), _slab(wvt)],
        grid=(grp.n_tiles,),
        out_specs=out_specs,
        out_shape=out_shape,
        compiler_params=_params(1),
        name=f"abproj_{grp.name}_{layer}",
    )


def _ctxkv_kernel(ckv_ref, kr_ref, wuk_ref, wvt_ref, k_ref, vt_ref):
    ckv_b = ckv_ref[...].astype(BF16)
    k_ref[...] = (_dot(ckv_b, wuk_ref[...]) + _tile_heads(kr_ref[...], MLA_HEADS)).astype(BF16)
    vt_ref[...] = _value_rows_t(ckv_b, wvt_ref).astype(BF16)


def _ctxkv(layer_idx, cache_ckv, cache_kr_padded, wuk, wvt):
    b, _, past, kvl = cache_ckv.shape
    return pl.pallas_call(
        _ctxkv_kernel,
        grid=(b,),
        in_specs=[pl.BlockSpec((None, None, past, kvl), lambda i: (i, layer_idx, 0, 0)),
                  pl.BlockSpec((None, None, past, LANE), lambda i: (i, layer_idx, 0, 0)),
                  _resident(wuk.shape), _resident(wvt.shape)],
        out_specs=[pl.BlockSpec((past, _HW), lambda i: (i, 0)), pl.BlockSpec((_HW, past), lambda i: (0, i))],
        out_shape=[jax.ShapeDtypeStruct((b * past, _HW), BF16), jax.ShapeDtypeStruct((_HW, b * past), BF16)],
        compiler_params=_params(1),
        name=f"ctxkv_{layer_idx}",
    )(cache_ckv, cache_kr_padded, wuk, wvt)


def _attn_kernel(*refs, n_main, n_ctx):
    if n_ctx:
        q_ref, k_ref, vt_ref, kc_ref, vtc_ref, o_ref, s_scr, p_scr = refs
    else:
        q_ref, k_ref, vt_ref, o_ref, s_scr, p_scr = refs
        kc_ref = vtc_ref = None
    tq = q_ref.shape[0]
    kc = KEY_CHUNK
    chunks = [(kc_ref, vtc_ref, j) for j in range(n_ctx)] + [(k_ref, vt_ref, j) for j in range(n_main)]
    nc = len(chunks)
    score_starts = list(range(0, n_ctx, SCORE_CHUNKS)) + list(range(n_ctx, nc, SCORE_CHUNKS))

    def score_block(h, c0, mpart):
        kref, _, j = chunks[c0]
        n = min(SCORE_CHUNKS, (n_ctx if c0 < n_ctx else nc) - c0)
        hs = slice(h * LANE, (h + 1) * LANE)
        s = _dot_nt(kref[j * kc:(j + n) * kc, hs], q_ref[:, hs])
        s_scr[h % 2, c0 * kc:(c0 + n) * kc, :] = s
        part = jnp.max(s.reshape(n * kc // 8, 8, tq), axis=0)
        return part if mpart is None else jnp.maximum(mpart, part)

    def exp_chunk(h, c, m):
        rows = slice(c * kc, (c + 1) * kc)
        p_scr[h % 2, rows, :] = jnp.exp2(s_scr[h % 2, rows, :] - m).astype(BF16)

    def value_chunk(h, c, acc):
        _, vref, j = chunks[c]
        d = _dot(vref[h * LANE:(h + 1) * LANE, j * kc:(j + 1) * kc], p_scr[h % 2, c * kc:(c + 1) * kc, :])
        return d if acc is None else acc + d

    row_max = {}
    for h in range(MLA_HEADS + 2):
        mpart, acc = None, None
        for c in range(nc):
            if h < MLA_HEADS and c in score_starts:
                mpart = score_block(h, c, mpart)
            if 1 <= h <= MLA_HEADS:
                exp_chunk(h - 1, c, row_max[h - 1])
            if h >= 2:
                acc = value_chunk(h - 2, c, acc)
        if h >= 2:
            inv = 1.0 / acc[MLA_V:MLA_V + 1, :]
            o_ref[(h - 2) * MLA_V:(h - 1) * MLA_V, :] = (acc[:MLA_V, :] * inv).astype(o_ref.dtype)
        if h < MLA_HEADS:
            row_max[h] = jnp.max(mpart, axis=0, keepdims=True)


def _attention(name, q, k, vt, n_seq, seq_len, ctx=None):
    tq = min(ATTN_Q_TILE, seq_len)
    qt = seq_len // tq
    ow = MLA_HEADS * MLA_V
    mode = dict(pipeline_mode=pl.Buffered(1)) if qt > 1 else {}
    in_specs = [pl.BlockSpec((tq, _HW), lambda b, i: (b * qt + i, 0)),
                pl.BlockSpec((seq_len, _HW), lambda b, i: (b, 0), **mode),
                pl.BlockSpec((_HW, seq_len), lambda b, i: (0, b), **mode)]
    args = [q, k, vt]
    past = 0
    if ctx is not None:
        kc, vtc, past = ctx
        in_specs += [pl.BlockSpec((past, _HW), lambda b, i: (b, 0), **mode),
                     pl.BlockSpec((_HW, past), lambda b, i: (0, b), **mode)]
        args += [kc, vtc]
    assert seq_len % KEY_CHUNK == 0 and past % KEY_CHUNK == 0
    return pl.pallas_call(
        functools.partial(_attn_kernel, n_main=seq_len // KEY_CHUNK, n_ctx=past // KEY_CHUNK),
        grid=(n_seq, qt),
        in_specs=in_specs,
        out_specs=pl.BlockSpec((ow, tq), lambda b, i: (0, b * qt + i)),
        out_shape=jax.ShapeDtypeStruct((ow, n_seq * seq_len), BF16),
        scratch_shapes=[pltpu.VMEM((2, seq_len + past, tq), F32), pltpu.VMEM((2, seq_len + past, tq), BF16)],
        compiler_params=_params(2),
        name=name,
    )(*args)


_RET_PAIR = 2
_RET_PW = _RET_PAIR * RET_DK
RET_UNROLL = 8
RET_BODY_UNITS = 16


def _ret_kernel(*refs, n_local, n_chunks, unroll, has_s0, has_prev, want_final):
    refs = list(refs)
    ld_ref, q_ref, k_ref, v_ref, g_ref = refs[:5]
    refs = refs[5:]
    s0_ref = refs.pop(0) if has_s0 else None
    if has_prev:
        refs.pop(0)
    o_ref = refs.pop(0)
    sfin_ref = refs.pop(0) if want_final else None
    cross_ref, sf_ref, sb_ref = refs
    seq_len = n_chunks * CHUNK

    ii = lax.broadcasted_iota(jnp.int32, (CHUNK, CHUNK), 0).astype(F32)
    jj = lax.broadcasted_iota(jnp.int32, (CHUNK, CHUNK), 1).astype(F32)
    rel = ii - jj
    consts = []
    for h in range(_RET_PAIR):
        lg_f = -jnp.exp(ld_ref[0, h])
        lg_b = -jnp.exp(ld_ref[1, h])
        mask = (jnp.where(rel >= 0, jnp.exp(lg_f * jnp.maximum(rel, 0.0)), 0.0)
                + jnp.where(rel <= 0, jnp.exp(lg_b * jnp.maximum(-rel, 0.0)), 0.0))
        consts.append(dict(
            mask=mask,
            qd_f=jnp.exp(lg_f * (ii + 1.0)), kd_f=jnp.exp(lg_f * (CHUNK - 1.0 - ii)), cd_f=jnp.exp(lg_f * CHUNK),
            qd_b=jnp.exp(lg_b * (CHUNK - ii)), kd_b=jnp.exp(lg_b * ii), cd_b=jnp.exp(lg_b * CHUNK)))
    for s in range(n_local):
        for h in range(_RET_PAIR):
            if has_s0:
                sf_ref[s, h] = s0_ref[s, 0, h]
                sb_ref[s, h] = s0_ref[s, 1, h]
            else:
                sf_ref[s, h] = jnp.zeros((RET_DK, RET_DV), F32)
                sb_ref[s, h] = jnp.zeros((RET_DK, RET_DV), F32)

    units = [(s, h, u) for s in range(n_local) for h in range(_RET_PAIR) for u in range(unroll)]

    def rows_of(s, n):
        start = s * seq_len + n * CHUNK
        if isinstance(start, int):
            return slice(start, start + CHUNK)
        return pl.ds(pl.multiple_of(start, CHUNK), CHUNK)

    def head_lanes(h):
        return slice(h * RET_DK, (h + 1) * RET_DK)

    def bwd_block(t):
        loaded, kv = {}, {}
        for (s, h, u) in units:
            rows, hs = rows_of(s, n_chunks - 1 - (t * unroll + u)), head_lanes(h)
            loaded[s, h, u] = (rows, hs, q_ref[rows, hs])
            kv[s, h, u] = _dot_tn((k_ref[rows, hs].astype(F32) * consts[h]["kd_b"]).astype(BF16), v_ref[rows, hs])
        for s in range(n_local):
            for h in range(_RET_PAIR):
                c = consts[h]
                state = sb_ref[s, h]
                for u in range(unroll):
                    rows, hs, q = loaded[s, h, u]
                    cross_ref[rows, hs] = _dot(q, state.astype(BF16)) * c["qd_b"]
                    state = state * c["cd_b"] + kv[s, h, u]
                sb_ref[s, h] = state

    def fwd_block(t):
        loaded = {}
        for (s, h, u) in units:
            rows, hs = rows_of(s, t * unroll + u), head_lanes(h)
            loaded[s, h, u] = (rows, hs, q_ref[rows, hs], k_ref[rows, hs], v_ref[rows, hs])
        qk, kv = {}, {}
        for key in units:
            rows, hs, q, k, v = loaded[key]
            qk[key] = _dot_nt(q, k)
            kv[key] = _dot_tn((k.astype(F32) * consts[key[1]]["kd_f"]).astype(BF16), v)
        inner, cross = {}, {}
        for s in range(n_local):
            for h in range(_RET_PAIR):
                c = consts[h]
                state = sf_ref[s, h]
                for u in range(unroll):
                    rows, hs, q, k, v = loaded[s, h, u]
                    inner[s, h, u] = _dot((qk[s, h, u] * c["mask"]).astype(BF16), v)
                    cross[s, h, u] = _dot(q, state.astype(BF16))
                    state = state * c["cd_f"] + kv[s, h, u]
                sf_ref[s, h] = state
        for key in units:
            rows, hs = loaded[key][:2]
            o = inner[key] + cross[key] * consts[key[1]]["qd_f"] + cross_ref[rows, hs]
            gate = g_ref[rows, hs]
            o_ref[rows, hs] = (_rms(o) * (gate * jax.nn.sigmoid(gate))).astype(o_ref.dtype)

    n_blocks = n_chunks // unroll
    if n_blocks == 1:
        bwd_block(0)
        fwd_block(0)
    else:
        def loop_body(block):
            def body(t, carry):
                block(t)
                return carry
            return body
        lax.fori_loop(0, n_blocks, loop_body(bwd_block), 0)
        lax.fori_loop(0, n_blocks, loop_body(fwd_block), 0)
    if want_final:
        for s in range(n_local):
            for h in range(_RET_PAIR):
                sfin_ref[s, 0, h] = sf_ref[s, h]
                sfin_ref[s, 1, h] = sb_ref[s, h]


def _retention(name, j, ld, rq, rk, rv, rg, n_seq, seq_len, s0=None, want_final=False, finals=None):
    pairs = RET_HEADS // _RET_PAIR
    n_chunks = seq_len // CHUNK
    n_layers = ld.shape[0]
    unroll = min(RET_UNROLL, n_chunks)
    n_local = max(1, RET_BODY_UNITS // n_chunks)
    assert n_chunks % unroll == 0 and n_seq % n_local == 0
    seq = pl.BlockSpec((n_local * seq_len, _RET_PW), lambda b, p: (b, p))
    state = pl.BlockSpec((n_local, None, 2, _RET_PAIR, RET_DK, RET_DV), lambda b, p: (b, j, 0, p, 0, 0))
    operands = [(ld, pl.BlockSpec((None, 2, _RET_PAIR, CHUNK, CHUNK), lambda b, p: (j, 0, p, 0, 0))),
                (rq, seq), (rk, seq), (rv, seq), (rg, seq)]
    if s0 is not None:
        operands.append((s0, state))
    aliases = {}
    if finals is not None:
        aliases = {len(operands): 1}
        operands.append((finals, pl.BlockSpec(memory_space=pl.ANY)))
    out_specs = [seq]
    out_shape = [jax.ShapeDtypeStruct((n_seq * seq_len, _RW), BF16)]
    if want_final:
        out_specs.append(state)
        out_shape.append(jax.ShapeDtypeStruct((n_seq, n_layers, 2, RET_HEADS, RET_DK, RET_DV), F32))
    res = _call(
        functools.partial(_ret_kernel, n_local=n_local, n_chunks=n_chunks, unroll=unroll,
                          has_s0=s0 is not None, has_prev=finals is not None, want_final=want_final),
        operands,
        grid=(n_seq // n_local, pairs),
        out_specs=out_specs,
        out_shape=out_shape,
        input_output_aliases=aliases,
        scratch_shapes=[pltpu.VMEM((n_local * seq_len, _RET_PW), F32),
                        pltpu.VMEM((n_local, _RET_PAIR, RET_DK, RET_DV), F32),
                        pltpu.VMEM((n_local, _RET_PAIR, RET_DK, RET_DV), F32)],
        compiler_params=_params(2),
        name=name,
    )
    return res if want_final else res[0]


def _merge_ffn_kernel(x_ref, mod_ref, oat_ref, ret_ref, woa_ref, wor_ref, g_ref, wg_ref, wu_ref, wd_ref, o_ref):
    n_sub = x_ref.shape[0] // FFN_SUB
    mixed = []
    for i in range(n_sub):
        rows = _sub_rows(i)
        y = _dot_tn(oat_ref[:, rows], woa_ref[...]) + _dot(ret_ref[rows, :], wor_ref[...])
        mixed.append(x_ref[rows, :] + mod_ref[0, 0, 2:3, :] * y)

    def emit(i, y):
        o_ref[_sub_rows(i), :] = y

    _swiglu_pipeline(lambda i: mixed[i], n_sub, mod_ref, g_ref, wg_ref, wu_ref, wd_ref, emit)


def _merge_ffn(grp, layer, j, x, mods, o_attn_t, ret, w):
    t, d = x.shape
    aw = o_attn_t.shape[0]
    assert aw == ret.shape[1]
    return _call(
        _merge_ffn_kernel,
        [(x, grp.rows(d)), (mods, grp.mods(layer, d)), (o_attn_t, grp.cols(aw)), (ret, grp.rows(aw)),
         _slab(w["o_ab"], j, rows=(0, aw)), _slab(w["o_ab"], j, rows=(1, aw)), _slab(w["norm_g"], layer, 1),
         _slab(w["ffn_gate"], layer), _slab(w["ffn_up"], layer), _slab(w["ffn_down"], layer)],
        grid=(grp.n_tiles,),
        out_specs=grp.rows(d),
        out_shape=jax.ShapeDtypeStruct((t, d), F32),
        compiler_params=_params(1),
        name=f"merge_ffn_{grp.name}_{layer}",
    )


def _rope_table(dec_seq, tm):
    f32 = np.float32
    rows = dec_seq // GRID_W
    row = np.repeat(np.arange(rows), GRID_W).astype(f32)
    col = np.tile(np.arange(GRID_W), rows).astype(f32)
    half = MLA_ROPE // 2
    inv = (1.0 / np.power(f32(ROPE_BASE), np.arange(0, half, 2, dtype=f32) / f32(half))).astype(f32)
    ang = np.stack([row[:, None] * inv, col[:, None] * inv], axis=1)
    ang = np.stack([ang, ang], axis=2).reshape(dec_seq, MLA_ROPE)
    cos, sin = np.cos(ang).astype(f32), np.sin(ang).astype(f32)
    lo = MLA_NOPE + MLA_ROPE
    ones, zeros = np.ones((dec_seq, lo), f32), np.zeros((dec_seq, lo), f32)
    pos = np.concatenate([ones, cos, zeros, sin, zeros, cos, zeros, sin], axis=1)
    flat_k = np.concatenate([np.zeros((tm, MLA_NOPE), f32), np.ones((tm, MLA_ROPE), f32),
                             np.zeros((tm, MLA_ROPE), f32)], axis=1)
    flat = np.concatenate([np.ones((tm, LANE), f32), np.zeros((tm, LANE), f32), flat_k,
                           np.zeros((tm, LANE), f32)], axis=1)
    return jnp.asarray(np.concatenate([flat, pos], axis=0))


def _layout_w_in(w):
    w = w.astype(BF16)
    d = w.shape[0]
    o = 0
    parts = {}
    for name, width in (("cq", Q_LORA), ("ckv", KV_LORA), ("kr", MLA_ROPE), ("rq", _RW), ("rk", _RW),
                        ("rv", _RW), ("rg", _RW)):
        parts[name] = w[:, o:o + width]
        o += width
    kr = parts["kr"]
    kr_a = jnp.concatenate([jnp.zeros((d, MLA_NOPE), w.dtype), kr, kr], axis=1)
    kr_b = jnp.concatenate([jnp.zeros((d, MLA_NOPE + MLA_ROPE), w.dtype), _rotate_half_axial(kr)], axis=1)
    return jnp.concatenate([parts["cq"], parts["ckv"], parts["rq"], parts["rk"], parts["rv"], parts["rg"],
                            kr_a, kr_b], axis=1).astype(BF16)


def _layout_w_uq(w):
    w = w.astype(BF16)
    r = w.shape[0]
    wh = w.reshape(r, MLA_HEADS, MLA_NOPE + MLA_ROPE)
    nope, rope = wh[..., :MLA_NOPE], wh[..., MLA_NOPE:]
    main = jnp.concatenate([nope, rope, rope], axis=-1).reshape(r, _HW)
    rot = jnp.concatenate([jnp.zeros(nope.shape[:2] + (MLA_NOPE + MLA_ROPE,), w.dtype),
                           _rotate_half_axial(rope)], axis=-1).reshape(r, _HW)
    return jnp.concatenate([main, rot], axis=1).astype(BF16)


def _layout_w_ukv(w):
    w = w.astype(BF16)
    r = w.shape[0]
    wh = w.reshape(r, MLA_HEADS, MLA_NOPE + MLA_V)
    kn, v = wh[..., :MLA_NOPE], wh[..., MLA_NOPE:]
    wuk = jnp.concatenate([kn, jnp.zeros((r, MLA_HEADS, LANE - MLA_NOPE), w.dtype)], axis=-1).reshape(r, _HW)
    wv = jnp.concatenate([v, jnp.zeros((r, MLA_HEADS, LANE - MLA_V), w.dtype)], axis=-1).reshape(r, _HW)
    return wuk.astype(BF16), wv.T.astype(BF16)


def kernel(x_prompt, x_sample, cache_mla_ckv, cache_mla_krope, state_ret, c, c_ctx, w_ada, b_ada, norm_g, w_in_ab, q_norm_g, w_uq, kv_norm_g, w_ukv, ret_log_decay, w_o_ab, w_in_c, ln_g_c, ln_b_c, w_s_c, b_s_c, w_out_c, w_ffn_gate, w_ffn_up, w_ffn_down, final_norm_g):
    batch, seq, d = x_prompt.shape
    dec_batch, dec_seq, _ = x_sample.shape
    depth = w_ada.shape[0]
    assert depth % 2 == 0
    past = cache_mla_ckv.shape[2]
    groups = [_Group("p", batch, seq, 0, positional=False), _Group("s", dec_batch, dec_seq, 1, positional=True)]
    ffn_groups = [grp.with_tile(FFN_ROW_TILE) for grp in groups]
    xs = [x_prompt.reshape(batch * seq, d), x_sample.reshape(dec_batch * dec_seq, d)]

    cond8 = jnp.zeros((8, d), F32).at[0].set(c_ctx).at[1:1 + dec_batch].set(c)
    mods = _ada_mods(cond8, w_ada, b_ada)

    table = _rope_table(dec_seq, ROW_TILE)
    cache_kr_padded = jnp.pad(cache_mla_krope, ((0, 0), (0, 0), (0, 0), (MLA_NOPE, LANE - MLA_NOPE - MLA_ROPE)))
    n_ab = w_in_ab.shape[0]
    gw = w_out_c.shape[1] // GM_GROUPS
    w = dict(
        norm_g=norm_g.reshape(depth, 2, 1, d), final_g=final_norm_g.reshape(1, d),
        ffn_gate=w_ffn_gate.astype(BF16), ffn_up=w_ffn_up.astype(BF16), ffn_down=w_ffn_down.astype(BF16),
        q_norm_g=q_norm_g.reshape(n_ab, 1, -1), kv_norm_g=kv_norm_g.reshape(n_ab, 1, -1), o_ab=w_o_ab.astype(BF16),
        in_c=w_in_c.astype(BF16), out_c=w_out_c.astype(BF16), s_c=w_s_c.astype(BF16),
        ln_g_c=ln_g_c[:, None, :], ln_b_c=ln_b_c[:, None, :],
        bs_c=jnp.broadcast_to(b_s_c[:, :, :, None], b_s_c.shape + (gw,)))
    ld = jnp.broadcast_to(ret_log_decay[:, :, :, None, None], ret_log_decay.shape + (CHUNK, CHUNK))

    ckv_out, kr_out, finals = [], [], None
    for l in range(depth):
        j = l // 2
        last = l == depth - 1
        if l % 2 == 0:
            win, wuq = _layout_w_in(w_in_ab[j]), _layout_w_uq(w_uq[j])
            wuk, wvt = _layout_w_ukv(w_ukv[j])
            kc, vtc = _ctxkv(j, cache_mla_ckv, cache_kr_padded, wuk, wvt)
            for gi, grp in enumerate(groups):
                prompt = gi == 0
                outs = _abproj(grp, l, j, xs[gi], mods, w, table, win, wuq, wuk, wvt, with_cache=prompt)
                q, k, vt, rq, rk, rv, rg = outs[:7]
                if prompt:
                    oat = _attention(f"attn_p_{l}", q, k, vt, grp.n_seq, grp.seq_len)
                    ret, finals = _retention(f"ret_p_{l}", j, ld, rq, rk, rv, rg, grp.n_seq, grp.seq_len,
                                             want_final=True, finals=finals)
                    ckv_out.append(outs[7].reshape(batch, seq, KV_LORA))
                    kr_out.append(outs[8][:, MLA_NOPE:MLA_NOPE + MLA_ROPE].reshape(batch, seq, MLA_ROPE))
                else:
                    oat = _attention(f"attn_s_{l}", q, k, vt, grp.n_seq, grp.seq_len, ctx=(kc, vtc, past))
                    ret = _retention(f"ret_s_{l}", j, ld, rq, rk, rv, rg, grp.n_seq, grp.seq_len, s0=state_ret)
                xs[gi] = _merge_ffn(ffn_groups[gi], l, j, xs[gi], mods, oat, ret, w)
        else:
            for gi, grp in enumerate(groups):
                x1 = _cmix(grp, l, j, xs[gi], mods, w)
                xs[gi] = _ffn(ffn_groups[gi], l, x1, mods, w, final_norm=last)

    y_prompt = xs[0].reshape(batch, seq, d)
    y_sample = xs[1].reshape(dec_batch, dec_seq, d)
    return (y_prompt, y_sample, jnp.stack(ckv_out, axis=1), jnp.stack(kr_out, axis=1), finals)
```

```python
import functools
import math

import jax
import jax.numpy as jnp
import numpy as np
from jax import lax
from jax.experimental import pallas as pl
from jax.experimental.pallas import tpu as pltpu

F32 = jnp.float32
BF16 = jnp.bfloat16
EPS = 1e-6

MLA_HEADS = 8
MLA_NOPE = 64
MLA_ROPE = 32
MLA_V = 64
Q_LORA = 256
KV_LORA = 128
ROPE_BASE = 10000.0
GRID_W = 64
RET_HEADS = 4
RET_DK = 128
RET_DV = 128
CHUNK = 128
GM_GROUPS = 8

LANE = 128
ROW_TILE = 512
ATTN_Q_TILE = 512
KEY_CHUNK = 256
SCORE_CHUNKS = 4
BF16_SUBLANES = 16
PV_ROWS = -(-(MLA_V + 1) // BF16_SUBLANES) * BF16_SUBLANES
VMEM_LIMIT = 56 * 1024 * 1024


def _params(n_axes):
    return pltpu.CompilerParams(dimension_semantics=("arbitrary",) * n_axes,
                                vmem_limit_bytes=VMEM_LIMIT)


def _resident(shape):
    nd = len(shape)
    return pl.BlockSpec(shape, lambda *_: (0,) * nd, pipeline_mode=pl.Buffered(1))


def _slab(arr, *lead, rows=None):
    shape = arr.shape[len(lead):]
    first = 0
    if rows is not None:
        first, n = rows
        shape = (n,) + shape[1:]
    idx = tuple(lead) + (first,) + (0,) * (len(shape) - 1)
    return arr, pl.BlockSpec((None,) * len(lead) + shape, lambda *_: idx, pipeline_mode=pl.Buffered(1))


def _call(kernel_fn, operands, **kw):
    return pl.pallas_call(kernel_fn, in_specs=[s for _, s in operands], **kw)(*[a for a, _ in operands])


def _rms(x):
    return x * lax.rsqrt(jnp.mean(x * x, axis=-1, keepdims=True) + EPS)


def _dot(a, b):
    return jnp.dot(a, b, preferred_element_type=F32)


def _dot_nt(a, b):
    return lax.dot_general(a, b, (((1,), (1,)), ((), ())), preferred_element_type=F32)


def _dot_tn(a, b):
    return lax.dot_general(a, b, (((0,), (0,)), ((), ())), preferred_element_type=F32)


def _tile_heads(x, n):
    return jnp.concatenate([x] * n, axis=1)


def _rotate_half_axial(x):
    xs = x.reshape(x.shape[:-1] + (2, 2, MLA_ROPE // 4))
    return jnp.stack([-xs[..., 1, :], xs[..., 0, :]], axis=-2).reshape(x.shape)


def _ada_kernel(cond_ref, w_ref, b_ref, o_ref):
    c = cond_ref[...]
    a = (c * jax.nn.sigmoid(c)).astype(BF16)
    o_ref[0] = _dot(a, w_ref[0].astype(BF16)) + b_ref[0]


def _ada_mods(cond8, w_ada, b_ada):
    depth, d, n = w_ada.shape
    tn = n // 4
    out = pl.pallas_call(
        _ada_kernel,
        grid=(depth, n // tn),
        in_specs=[pl.BlockSpec((8, d), lambda l, j: (0, 0)),
                  pl.BlockSpec((1, d, tn), lambda l, j: (l, 0, j)),
                  pl.BlockSpec((1, 1, tn), lambda l, j: (l, 0, j))],
        out_specs=pl.BlockSpec((1, 8, tn), lambda l, j: (l, 0, j)),
        out_shape=jax.ShapeDtypeStruct((depth, 8, n), F32),
        compiler_params=_params(2),
        name="ada_mod",
    )(cond8, w_ada, b_ada.reshape(depth, 1, n))
    return out.reshape(depth, 8, 6, d)


class _Group:
    def __init__(self, name, n_seq, seq_len, cond_base, positional, tm=ROW_TILE):
        assert seq_len % tm == 0 or tm % seq_len == 0
        self.name = name
        self.n_seq, self.seq_len = n_seq, seq_len
        self.n_rows = n_seq * seq_len
        self.tm = tm
        self.n_tiles = self.n_rows // tm
        self.cond_base = cond_base
        self.tiles_per_cond = seq_len // tm if positional else self.n_tiles
        self.positional = positional

    def with_tile(self, tm):
        return _Group(self.name, self.n_seq, self.seq_len, self.cond_base, self.positional, tm)

    def cond(self, i):
        return self.cond_base + i // self.tiles_per_cond

    def table(self, i):
        return 1 + i % self.tiles_per_cond if self.positional else 0

    def rows(self, width):
        return pl.BlockSpec((self.tm, width), lambda i: (i, 0))

    def cols(self, height):
        return pl.BlockSpec((height, self.tm), lambda i: (0, i))

    def mods(self, layer, d):
        return pl.BlockSpec((1, 1, 6, d), lambda i: (layer, self.cond(i), 0, 0))


def _modulated(x, g, shift, scale):
    return (_rms(x) * g) * (1.0 + scale) + shift


FFN_ROW_TILE = 1024
FFN_SUB = 256


def _swiglu_pipeline(x_of, n_sub, mod_ref, g_ref, wg_ref, wu_ref, wd_ref, emit):
    def project(i):
        x = x_of(i)
        hb = _modulated(x, g_ref[...], mod_ref[0, 0, 3:4, :], mod_ref[0, 0, 4:5, :]).astype(BF16)
        return x, _dot(hb, wg_ref[...]), _dot(hb, wu_ref[...])

    def finish(i, x, gate, up):
        act = (gate * jax.nn.sigmoid(gate) * up).astype(BF16)
        emit(i, x + mod_ref[0, 0, 5:6, :] * _dot(act, wd_ref[...]))

    cur = project(0)
    for i in range(n_sub):
        nxt = project(i + 1) if i + 1 < n_sub else None
        finish(i, *cur)
        cur = nxt


def _sub_rows(i):
    return slice(i * FFN_SUB, (i + 1) * FFN_SUB)


def _ffn_kernel(x_ref, mod_ref, g_ref, wg_ref, wu_ref, wd_ref, fg_ref, o_ref, *, final_norm):
    def emit(i, y):
        if final_norm:
            y = _rms(y) * fg_ref[...]
        o_ref[_sub_rows(i), :] = y

    _swiglu_pipeline(lambda i: x_ref[_sub_rows(i), :], x_ref.shape[0] // FFN_SUB,
                     mod_ref, g_ref, wg_ref, wu_ref, wd_ref, emit)


def _ffn(grp, layer, x, mods, w, final_norm):
    t, d = x.shape
    return _call(
        functools.partial(_ffn_kernel, final_norm=final_norm),
        [(x, grp.rows(d)), (mods, grp.mods(layer, d)), _slab(w["norm_g"], layer, 1),
         _slab(w["ffn_gate"], layer), _slab(w["ffn_up"], layer), _slab(w["ffn_down"], layer), _slab(w["final_g"])],
        grid=(grp.n_tiles,),
        out_specs=grp.rows(d),
        out_shape=jax.ShapeDtypeStruct((t, d), F32),
        compiler_params=_params(1),
        name=f"ffn_{grp.name}_{layer}",
    )


CMIX_SUB = 256


def _cmix_kernel(x_ref, mod_ref, g_ref, win_ref, lng_ref, lnb_ref, ws_ref, bs_ref, wout_ref, o_ref, gated_ref):
    tm = x_ref.shape[0]
    w = lng_ref.shape[1]
    gw = w // GM_GROUPS

    def project(i):
        x = x_ref[i * CMIX_SUB:(i + 1) * CMIX_SUB, :]
        hb = _modulated(x, g_ref[...], mod_ref[0, 0, 0:1, :], mod_ref[0, 0, 1:2, :]).astype(BF16)
        return _dot(hb, win_ref[...])

    def gate(i, uv):
        uv = jax.nn.gelu(uv)
        u = uv[:, :w]
        v = uv[:, w:]
        mu = jnp.mean(v, axis=-1, keepdims=True)
        vc = v - mu
        var = jnp.mean(vc * vc, axis=-1, keepdims=True)
        vb = (vc * lax.rsqrt(var + EPS) * lng_ref[...] + lnb_ref[...]).astype(BF16)
        for r in range(CMIX_SUB // CHUNK):
            rs = slice(r * CHUNK, (r + 1) * CHUNK)
            out_rows = slice(i * CMIX_SUB + r * CHUNK, i * CMIX_SUB + (r + 1) * CHUNK)
            for gi in range(GM_GROUPS):
                cs = slice(gi * gw, (gi + 1) * gw)
                mixed = _dot(ws_ref[gi], vb[rs, cs]) + bs_ref[gi]
                gated_ref[out_rows, cs] = (u[rs, cs] * mixed).astype(BF16)

    def project_out(i):
        rows = slice(i * CMIX_SUB, (i + 1) * CMIX_SUB)
        o_ref[rows, :] = x_ref[rows, :] + mod_ref[0, 0, 2:3, :] * _dot(gated_ref[rows, :], wout_ref[...])

    n_sub = tm // CMIX_SUB
    uv = project(0)
    for i in range(n_sub):
        uv_next = project(i + 1) if i + 1 < n_sub else None
        gate(i, uv)
        project_out(i)
        uv = uv_next


def _cmix(grp, layer, j, x, mods, w):
    t, d = x.shape
    width = w["out_c"].shape[1]
    return _call(
        _cmix_kernel,
        [(x, grp.rows(d)), (mods, grp.mods(layer, d)), _slab(w["norm_g"], layer, 0), _slab(w["in_c"], j),
         _slab(w["ln_g_c"], j), _slab(w["ln_b_c"], j), _slab(w["s_c"], j), _slab(w["bs_c"], j), _slab(w["out_c"], j)],
        grid=(grp.n_tiles,),
        out_specs=grp.rows(d),
        out_shape=jax.ShapeDtypeStruct((t, d), F32),
        scratch_shapes=[pltpu.VMEM((grp.tm, width), BF16)],
        compiler_params=_params(1),
        name=f"cmix_{grp.name}_{layer}",
    )


_HW = MLA_HEADS * LANE
_RW = RET_HEADS * RET_DK
_C_CQ = 0
_C_CKV = _C_CQ + Q_LORA
_C_RQ = _C_CKV + KV_LORA
_C_RK = _C_RQ + _RW
_C_RV = _C_RK + _RW
_C_RG = _C_RV + _RW
_C_KRA = _C_RG + _RW
_C_KRB = _C_KRA + LANE
_C_END = _C_KRB + LANE

_Q_SCALE = (MLA_NOPE + MLA_ROPE) ** -0.5 * math.log2(math.e)


def _value_rows_t(ckv_b, wvt_ref):
    vt = _dot_nt(wvt_ref[...], ckv_b)
    row = lax.broadcasted_iota(jnp.int32, vt.shape, 0)
    return jnp.where((row & (LANE - 1)) == MLA_V, 1.0, vt)


def _abproj_kernel(*refs, with_cache):
    (x_ref, mod_ref, g_ref, tab_ref, win_ref, qg_ref, wuq_ref, kvg_ref, wuk_ref, wvt_ref,
     q_ref, k_ref, vt_ref, rq_ref, rk_ref, rv_ref, rg_ref) = refs[:17]
    x = x_ref[...]
    hb = _modulated(x, g_ref[...], mod_ref[0, 0, 0:1, :], mod_ref[0, 0, 1:2, :]).astype(BF16)
    p = _dot(hb, win_ref[...])
    rq_ref[...] = p[:, _C_RQ:_C_RK].astype(BF16)
    rk_ref[...] = (p[:, _C_RK:_C_RV] * (RET_DK ** -0.5)).astype(BF16)
    rv_ref[...] = p[:, _C_RV:_C_RG].astype(BF16)
    rg_ref[...] = p[:, _C_RG:_C_KRA]
    kr_a = p[:, _C_KRA:_C_KRB]
    kr_b = p[:, _C_KRB:_C_END]

    tab = tab_ref[...]
    cos_q, sin_q = tab[:, 0:LANE], tab[:, LANE:2 * LANE]
    cos_k, sin_k = tab[:, 2 * LANE:3 * LANE], tab[:, 3 * LANE:4 * LANE]

    cqn = (_rms(p[:, _C_CQ:_C_CKV]) * qg_ref[...]).astype(BF16)
    q2 = _dot(cqn, wuq_ref[...])
    q = q2[:, :_HW] * _tile_heads(cos_q, MLA_HEADS) + q2[:, _HW:] * _tile_heads(sin_q, MLA_HEADS)
    q_ref[...] = (q * _Q_SCALE).astype(BF16)

    ckv = _rms(p[:, _C_CKV:_C_RQ]) * kvg_ref[...]
    ckv_b = ckv.astype(BF16)
    k_rope = kr_a * cos_k + kr_b * sin_k
    k_ref[...] = (_dot(ckv_b, wuk_ref[...]) + _tile_heads(k_rope, MLA_HEADS)).astype(BF16)
    vt_ref[...] = _value_rows_t(ckv_b, wvt_ref).astype(BF16)
    if with_cache:
        ckv_ref, kr_ref = refs[17:]
        ckv_ref[...] = ckv
        kr_ref[...] = kr_a


def _abproj(grp, layer, j, x, mods, w, table, win, wuq, wuk, wvt, with_cache):
    t, d = x.shape
    row_outs = [(_HW, BF16), (_HW, BF16), None, (_RW, BF16), (_RW, BF16), (_RW, BF16), (_RW, F32)]
    if with_cache:
        row_outs += [(KV_LORA, F32), (LANE, F32)]
    out_specs = [grp.cols(_HW) if o is None else grp.rows(o[0]) for o in row_outs]
    out_shape = [jax.ShapeDtypeStruct((_HW, t), BF16) if o is None else jax.ShapeDtypeStruct((t, o[0]), o[1])
                 for o in row_outs]
    return _call(
        functools.partial(_abproj_kernel, with_cache=with_cache),
        [(x, grp.rows(d)), (mods, grp.mods(layer, d)), _slab(w["norm_g"], layer, 0),
         (table, pl.BlockSpec((grp.tm, 4 * LANE), lambda i: (grp.table(i), 0))),
         _slab(win), _slab(w["q_norm_g"], j), _slab(wuq), _slab(w["kv_norm_g"], j), _slab(wuk), _slab(wvt)],
        grid=(grp.n_tiles,),
        out_specs=out_specs,
        out_shape=out_shape,
        compiler_params=_params(1),
        name=f"abproj_{grp.name}_{layer}",
    )


def _ctxkv_kernel(ckv_ref, kr_ref, wuk_ref, wvt_ref, k_ref, vt_ref):
    ckv_b = ckv_ref[...].astype(BF16)
    k_ref[...] = (_dot(ckv_b, wuk_ref[...]) + _tile_heads(kr_ref[...], MLA_HEADS)).astype(BF16)
    vt_ref[...] = _value_rows_t(ckv_b, wvt_ref).astype(BF16)


def _ctxkv(layer_idx, cache_ckv, cache_kr_padded, wuk, wvt):
    b, _, past, kvl = cache_ckv.shape
    return pl.pallas_call(
        _ctxkv_kernel,
        grid=(b,),
        in_specs=[pl.BlockSpec((None, None, past, kvl), lambda i: (i, layer_idx, 0, 0)),
                  pl.BlockSpec((None, None, past, LANE), lambda i: (i, layer_idx, 0, 0)),
                  _resident(wuk.shape), _resident(wvt.shape)],
        out_specs=[pl.BlockSpec((past, _HW), lambda i: (i, 0)), pl.BlockSpec((_HW, past), lambda i: (0, i))],
        out_shape=[jax.ShapeDtypeStruct((b * past, _HW), BF16), jax.ShapeDtypeStruct((_HW, b * past), BF16)],
        compiler_params=_params(1),
        name=f"ctxkv_{layer_idx}",
    )(cache_ckv, cache_kr_padded, wuk, wvt)


def _attn_kernel(*refs, n_main, n_ctx):
    if n_ctx:
        q_ref, k_ref, vt_ref, kc_ref, vtc_ref, o_ref, s_scr, p_scr = refs
    else:
        q_ref, k_ref, vt_ref, o_ref, s_scr, p_scr = refs
        kc_ref = vtc_ref = None
    tq = q_ref.shape[0]
    kc = KEY_CHUNK
    chunks = [(kc_ref, vtc_ref, j) for j in range(n_ctx)] + [(k_ref, vt_ref, j) for j in range(n_main)]
    nc = len(chunks)
    score_starts = list(range(0, n_ctx, SCORE_CHUNKS)) + list(range(n_ctx, nc, SCORE_CHUNKS))

    def score_block(h, c0, mpart):
        kref, _, j = chunks[c0]
        n = min(SCORE_CHUNKS, (n_ctx if c0 < n_ctx else nc) - c0)
        hs = slice(h * LANE, (h + 1) * LANE)
        s = _dot_nt(kref[j * kc:(j + n) * kc, hs], q_ref[:, hs])
        s_scr[h % 2, c0 * kc:(c0 + n) * kc, :] = s
        part = jnp.max(s.reshape(n * kc // 8, 8, tq), axis=0)
        return part if mpart is None else jnp.maximum(mpart, part)

    def exp_chunk(h, c, m):
        rows = slice(c * kc, (c + 1) * kc)
        p_scr[h % 2, rows, :] = jnp.exp2(s_scr[h % 2, rows, :] - m).astype(BF16)

    def value_chunk(h, c, acc):
        _, vref, j = chunks[c]
        d = _dot(vref[h * LANE:h * LANE + PV_ROWS, j * kc:(j + 1) * kc], p_scr[h % 2, c * kc:(c + 1) * kc, :])
        return d if acc is None else acc + d

    row_max = {}
    for h in range(MLA_HEADS + 2):
        mpart, acc = None, None
        for c in range(nc):
            if h < MLA_HEADS and c in score_starts:
                mpart = score_block(h, c, mpart)
            if 1 <= h <= MLA_HEADS:
                exp_chunk(h - 1, c, row_max[h - 1])
            if h >= 2:
                acc = value_chunk(h - 2, c, acc)
        if h >= 2:
            inv = 1.0 / acc[MLA_V:MLA_V + 1, :]
            o_ref[(h - 2) * MLA_V:(h - 1) * MLA_V, :] = (acc[:MLA_V, :] * inv).astype(o_ref.dtype)
        if h < MLA_HEADS:
            row_max[h] = jnp.max(mpart, axis=0, keepdims=True)


def _attention(name, q, k, vt, n_seq, seq_len, ctx=None):
    tq = min(ATTN_Q_TILE, seq_len)
    qt = seq_len // tq
    ow = MLA_HEADS * MLA_V
    mode = dict(pipeline_mode=pl.Buffered(1)) if qt > 1 else {}
    in_specs = [pl.BlockSpec((tq, _HW), lambda b, i: (b * qt + i, 0)),
                pl.BlockSpec((seq_len, _HW), lambda b, i: (b, 0), **mode),
                pl.BlockSpec((_HW, seq_len), lambda b, i: (0, b), **mode)]
    args = [q, k, vt]
    past = 0
    if ctx is not None:
        kc, vtc, past = ctx
        in_specs += [pl.BlockSpec((past, _HW), lambda b, i: (b, 0), **mode),
                     pl.BlockSpec((_HW, past), lambda b, i: (0, b), **mode)]
        args += [kc, vtc]
    assert seq_len % KEY_CHUNK == 0 and past % KEY_CHUNK == 0
    return pl.pallas_call(
        functools.partial(_attn_kernel, n_main=seq_len // KEY_CHUNK, n_ctx=past // KEY_CHUNK),
        grid=(n_seq, qt),
        in_specs=in_specs,
        out_specs=pl.BlockSpec((ow, tq), lambda b, i: (0, b * qt + i)),
        out_shape=jax.ShapeDtypeStruct((ow, n_seq * seq_len), BF16),
        scratch_shapes=[pltpu.VMEM((2, seq_len + past, tq), F32), pltpu.VMEM((2, seq_len + past, tq), BF16)],
        compiler_params=_params(2),
        name=name,
    )(*args)


_RET_PAIR = 2
_RET_PW = _RET_PAIR * RET_DK
RET_UNROLL = 8
RET_BODY_UNITS = 16


def _ret_kernel(*refs, n_local, n_chunks, unroll, has_s0, has_prev, want_final, layer_slot):
    refs = list(refs)
    ld_ref, q_ref, k_ref, v_ref, g_ref = refs[:5]
    refs = refs[5:]
    s0_ref = refs.pop(0) if has_s0 else None
    if has_prev:
        refs.pop(0)
    o_ref = refs.pop(0)
    sfin_ref = refs.pop(0) if want_final else None
    cross_ref, sf_ref, sb_ref = refs
    seq_len = n_chunks * CHUNK

    ii = lax.broadcasted_iota(jnp.int32, (CHUNK, CHUNK), 0).astype(F32)
    jj = lax.broadcasted_iota(jnp.int32, (CHUNK, CHUNK), 1).astype(F32)
    rel = ii - jj
    consts = []
    for h in range(_RET_PAIR):
        lg_f = -jnp.exp(ld_ref[0, h])
        lg_b = -jnp.exp(ld_ref[1, h])
        mask = (jnp.where(rel >= 0, jnp.exp(lg_f * jnp.maximum(rel, 0.0)), 0.0)
                + jnp.where(rel <= 0, jnp.exp(lg_b * jnp.maximum(-rel, 0.0)), 0.0))
        consts.append(dict(
            mask=mask,
            qd_f=jnp.exp(lg_f * (ii + 1.0)), kd_f=jnp.exp(lg_f * (CHUNK - 1.0 - ii)), cd_f=jnp.exp(lg_f * CHUNK),
            qd_b=jnp.exp(lg_b * (CHUNK - ii)), kd_b=jnp.exp(lg_b * ii), cd_b=jnp.exp(lg_b * CHUNK)))
    for s in range(n_local):
        for h in range(_RET_PAIR):
            if has_s0:
                sf_ref[s, h] = s0_ref[s, 0, h]
                sb_ref[s, h] = s0_ref[s, 1, h]
            else:
                sf_ref[s, h] = jnp.zeros((RET_DK, RET_DV), F32)
                sb_ref[s, h] = jnp.zeros((RET_DK, RET_DV), F32)

    units = [(s, h, u) for s in range(n_local) for h in range(_RET_PAIR) for u in range(unroll)]

    def rows_of(s, n):
        start = s * seq_len + n * CHUNK
        if isinstance(start, int):
            return slice(start, start + CHUNK)
        return pl.ds(pl.multiple_of(start, CHUNK), CHUNK)

    def head_lanes(h):
        return slice(h * RET_DK, (h + 1) * RET_DK)

    def bwd_block(t):
        loaded, kv = {}, {}
        for (s, h, u) in units:
            rows, hs = rows_of(s, n_chunks - 1 - (t * unroll + u)), head_lanes(h)
            loaded[s, h, u] = (rows, hs, q_ref[rows, hs])
            kv[s, h, u] = _dot_tn((k_ref[rows, hs].astype(F32) * consts[h]["kd_b"]).astype(BF16), v_ref[rows, hs])
        for s in range(n_local):
            for h in range(_RET_PAIR):
                c = consts[h]
                state = sb_ref[s, h]
                for u in range(unroll):
                    rows, hs, q = loaded[s, h, u]
                    cross_ref[rows, hs] = _dot(q, state.astype(BF16)) * c["qd_b"]
                    state = state * c["cd_b"] + kv[s, h, u]
                sb_ref[s, h] = state

    def fwd_block(t):
        loaded = {}
        for (s, h, u) in units:
            rows, hs = rows_of(s, t * unroll + u), head_lanes(h)
            loaded[s, h, u] = (rows, hs, q_ref[rows, hs], k_ref[rows, hs], v_ref[rows, hs])
        qk, kv = {}, {}
        for key in units:
            rows, hs, q, k, v = loaded[key]
            qk[key] = _dot_nt(q, k)
            kv[key] = _dot_tn((k.astype(F32) * consts[key[1]]["kd_f"]).astype(BF16), v)
        inner, cross = {}, {}
        for s in range(n_local):
            for h in range(_RET_PAIR):
                c = consts[h]
                state = sf_ref[s, h]
                for u in range(unroll):
                    rows, hs, q, k, v = loaded[s, h, u]
                    inner[s, h, u] = _dot((qk[s, h, u] * c["mask"]).astype(BF16), v)
                    cross[s, h, u] = _dot(q, state.astype(BF16))
                    state = state * c["cd_f"] + kv[s, h, u]
                sf_ref[s, h] = state
        for key in units:
            rows, hs = loaded[key][:2]
            o = inner[key] + cross[key] * consts[key[1]]["qd_f"] + cross_ref[rows, hs]
            gate = g_ref[rows, hs]
            o_ref[rows, hs] = (_rms(o) * (gate * jax.nn.sigmoid(gate))).astype(o_ref.dtype)

    n_blocks = n_chunks // unroll
    if n_blocks == 1:
        bwd_block(0)
        fwd_block(0)
    else:
        def loop_body(block):
            def body(t, carry):
                block(t)
                return carry
            return body
        lax.fori_loop(0, n_blocks, loop_body(bwd_block), 0)
        lax.fori_loop(0, n_blocks, loop_body(fwd_block), 0)
    if want_final:
        if has_prev:
            slot = sfin_ref
        else:
            for other in range(sfin_ref.shape[1]):
                if other != layer_slot:
                    sfin_ref[:, other] = jnp.zeros(sfin_ref.shape[:1] + sfin_ref.shape[2:], F32)
            slot = sfin_ref.at[:, layer_slot]
        for s in range(n_local):
            for h in range(_RET_PAIR):
                slot[s, 0, h] = sf_ref[s, h]
                slot[s, 1, h] = sb_ref[s, h]


def _retention(name, j, ld, rq, rk, rv, rg, n_seq, seq_len, s0=None, want_final=False, finals=None):
    pairs = RET_HEADS // _RET_PAIR
    n_chunks = seq_len // CHUNK
    n_layers = ld.shape[0]
    unroll = min(RET_UNROLL, n_chunks)
    n_local = max(1, RET_BODY_UNITS // n_chunks)
    assert n_chunks % unroll == 0 and n_seq % n_local == 0
    seq = pl.BlockSpec((n_local * seq_len, _RET_PW), lambda b, p: (b, p))
    state = pl.BlockSpec((n_local, None, 2, _RET_PAIR, RET_DK, RET_DV), lambda b, p: (b, j, 0, p, 0, 0))
    operands = [(ld, pl.BlockSpec((None, 2, _RET_PAIR, CHUNK, CHUNK), lambda b, p: (j, 0, p, 0, 0))),
                (rq, seq), (rk, seq), (rv, seq), (rg, seq)]
    if s0 is not None:
        operands.append((s0, state))
    aliases = {}
    if finals is not None:
        aliases = {len(operands): 1}
        operands.append((finals, pl.BlockSpec(memory_space=pl.ANY)))
    out_specs = [seq]
    out_shape = [jax.ShapeDtypeStruct((n_seq * seq_len, _RW), BF16)]
    if want_final:
        all_slots = pl.BlockSpec((n_local, n_layers, 2, _RET_PAIR, RET_DK, RET_DV), lambda b, p: (b, 0, 0, p, 0, 0))
        out_specs.append(state if finals is not None else all_slots)
        out_shape.append(jax.ShapeDtypeStruct((n_seq, n_layers, 2, RET_HEADS, RET_DK, RET_DV), F32))
    res = _call(
        functools.partial(_ret_kernel, n_local=n_local, n_chunks=n_chunks, unroll=unroll, has_s0=s0 is not None,
                          has_prev=finals is not None, want_final=want_final, layer_slot=j),
        operands,
        grid=(n_seq // n_local, pairs),
        out_specs=out_specs,
        out_shape=out_shape,
        input_output_aliases=aliases,
        scratch_shapes=[pltpu.VMEM((n_local * seq_len, _RET_PW), F32),
                        pltpu.VMEM((n_local, _RET_PAIR, RET_DK, RET_DV), F32),
                        pltpu.VMEM((n_local, _RET_PAIR, RET_DK, RET_DV), F32)],
        compiler_params=_params(2),
        name=name,
    )
    return res if want_final else res[0]


def _merge_ffn_kernel(x_ref, mod_ref, oat_ref, ret_ref, woa_ref, wor_ref, g_ref, wg_ref, wu_ref, wd_ref, o_ref):
    n_sub = x_ref.shape[0] // FFN_SUB
    mixed = []
    for i in range(n_sub):
        rows = _sub_rows(i)
        y = _dot_tn(oat_ref[:, rows], woa_ref[...]) + _dot(ret_ref[rows, :], wor_ref[...])
        mixed.append(x_ref[rows, :] + mod_ref[0, 0, 2:3, :] * y)

    def emit(i, y):
        o_ref[_sub_rows(i), :] = y

    _swiglu_pipeline(lambda i: mixed[i], n_sub, mod_ref, g_ref, wg_ref, wu_ref, wd_ref, emit)


def _merge_ffn(grp, layer, j, x, mods, o_attn_t, ret, w):
    t, d = x.shape
    aw = o_attn_t.shape[0]
    assert aw == ret.shape[1]
    return _call(
        _merge_ffn_kernel,
        [(x, grp.rows(d)), (mods, grp.mods(layer, d)), (o_attn_t, grp.cols(aw)), (ret, grp.rows(aw)),
         _slab(w["o_ab"], j, rows=(0, aw)), _slab(w["o_ab"], j, rows=(1, aw)), _slab(w["norm_g"], layer, 1),
         _slab(w["ffn_gate"], layer), _slab(w["ffn_up"], layer), _slab(w["ffn_down"], layer)],
        grid=(grp.n_tiles,),
        out_specs=grp.rows(d),
        out_shape=jax.ShapeDtypeStruct((t, d), F32),
        compiler_params=_params(1),
        name=f"merge_ffn_{grp.name}_{layer}",
    )


def _rope_table(dec_seq, tm):
    f32 = np.float32
    rows = dec_seq // GRID_W
    row = np.repeat(np.arange(rows), GRID_W).astype(f32)
    col = np.tile(np.arange(GRID_W), rows).astype(f32)
    half = MLA_ROPE // 2
    inv = (1.0 / np.power(f32(ROPE_BASE), np.arange(0, half, 2, dtype=f32) / f32(half))).astype(f32)
    ang = np.stack([row[:, None] * inv, col[:, None] * inv], axis=1)
    ang = np.stack([ang, ang], axis=2).reshape(dec_seq, MLA_ROPE)
    cos, sin = np.cos(ang).astype(f32), np.sin(ang).astype(f32)
    lo = MLA_NOPE + MLA_ROPE
    ones, zeros = np.ones((dec_seq, lo), f32), np.zeros((dec_seq, lo), f32)
    pos = np.concatenate([ones, cos, zeros, sin, zeros, cos, zeros, sin], axis=1)
    flat_k = np.concatenate([np.zeros((tm, MLA_NOPE), f32), np.ones((tm, MLA_ROPE), f32),
                             np.zeros((tm, MLA_ROPE), f32)], axis=1)
    flat = np.concatenate([np.ones((tm, LANE), f32), np.zeros((tm, LANE), f32), flat_k,
                           np.zeros((tm, LANE), f32)], axis=1)
    return jnp.asarray(np.concatenate([flat, pos], axis=0))


def _layout_w_in(w):
    w = w.astype(BF16)
    d = w.shape[0]
    o = 0
    parts = {}
    for name, width in (("cq", Q_LORA), ("ckv", KV_LORA), ("kr", MLA_ROPE), ("rq", _RW), ("rk", _RW),
                        ("rv", _RW), ("rg", _RW)):
        parts[name] = w[:, o:o + width]
        o += width
    kr = parts["kr"]
    kr_a = jnp.concatenate([jnp.zeros((d, MLA_NOPE), w.dtype), kr, kr], axis=1)
    kr_b = jnp.concatenate([jnp.zeros((d, MLA_NOPE + MLA_ROPE), w.dtype), _rotate_half_axial(kr)], axis=1)
    return jnp.concatenate([parts["cq"], parts["ckv"], parts["rq"], parts["rk"], parts["rv"], parts["rg"],
                            kr_a, kr_b], axis=1).astype(BF16)


def _layout_w_uq(w):
    w = w.astype(BF16)
    r = w.shape[0]
    wh = w.reshape(r, MLA_HEADS, MLA_NOPE + MLA_ROPE)
    nope, rope = wh[..., :MLA_NOPE], wh[..., MLA_NOPE:]
    main = jnp.concatenate([nope, rope, rope], axis=-1).reshape(r, _HW)
    rot = jnp.concatenate([jnp.zeros(nope.shape[:2] + (MLA_NOPE + MLA_ROPE,), w.dtype),
                           _rotate_half_axial(rope)], axis=-1).reshape(r, _HW)
    return jnp.concatenate([main, rot], axis=1).astype(BF16)


def _layout_w_ukv(w):
    w = w.astype(BF16)
    r = w.shape[0]
    wh = w.reshape(r, MLA_HEADS, MLA_NOPE + MLA_V)
    kn, v = wh[..., :MLA_NOPE], wh[..., MLA_NOPE:]
    wuk = jnp.concatenate([kn, jnp.zeros((r, MLA_HEADS, LANE - MLA_NOPE), w.dtype)], axis=-1).reshape(r, _HW)
    wv = jnp.concatenate([v, jnp.zeros((r, MLA_HEADS, LANE - MLA_V), w.dtype)], axis=-1).reshape(r, _HW)
    return wuk.astype(BF16), wv.T.astype(BF16)


def kernel(x_prompt, x_sample, cache_mla_ckv, cache_mla_krope, state_ret, c, c_ctx, w_ada, b_ada, norm_g, w_in_ab, q_norm_g, w_uq, kv_norm_g, w_ukv, ret_log_decay, w_o_ab, w_in_c, ln_g_c, ln_b_c, w_s_c, b_s_c, w_out_c, w_ffn_gate, w_ffn_up, w_ffn_down, final_norm_g):
    batch, seq, d = x_prompt.shape
    dec_batch, dec_seq, _ = x_sample.shape
    depth = w_ada.shape[0]
    assert depth % 2 == 0
    past = cache_mla_ckv.shape[2]
    groups = [_Group("p", batch, seq, 0, positional=False), _Group("s", dec_batch, dec_seq, 1, positional=True)]
    ffn_groups = [grp.with_tile(FFN_ROW_TILE) for grp in groups]
    xs = [x_prompt.reshape(batch * seq, d), x_sample.reshape(dec_batch * dec_seq, d)]

    cond8 = jnp.zeros((8, d), F32).at[0].set(c_ctx).at[1:1 + dec_batch].set(c)
    mods = _ada_mods(cond8, w_ada, b_ada)

    table = _rope_table(dec_seq, ROW_TILE)
    cache_kr_padded = jnp.pad(cache_mla_krope, ((0, 0), (0, 0), (0, 0), (MLA_NOPE, LANE - MLA_NOPE - MLA_ROPE)))
    n_ab = w_in_ab.shape[0]
    gw = w_out_c.shape[1] // GM_GROUPS
    w = dict(
        norm_g=norm_g.reshape(depth, 2, 1, d), final_g=final_norm_g.reshape(1, d),
        ffn_gate=w_ffn_gate.astype(BF16), ffn_up=w_ffn_up.astype(BF16), ffn_down=w_ffn_down.astype(BF16),
        q_norm_g=q_norm_g.reshape(n_ab, 1, -1), kv_norm_g=kv_norm_g.reshape(n_ab, 1, -1), o_ab=w_o_ab.astype(BF16),
        in_c=w_in_c.astype(BF16), out_c=w_out_c.astype(BF16), s_c=w_s_c.astype(BF16),
        ln_g_c=ln_g_c[:, None, :], ln_b_c=ln_b_c[:, None, :],
        bs_c=jnp.broadcast_to(b_s_c[:, :, :, None], b_s_c.shape + (gw,)))
    ld = jnp.broadcast_to(ret_log_decay[:, :, :, None, None], ret_log_decay.shape + (CHUNK, CHUNK))

    ckv_out, kr_out, finals = [], [], None
    for l in range(depth):
        j = l // 2
        last = l == depth - 1
        if l % 2 == 0:
            win, wuq = _layout_w_in(w_in_ab[j]), _layout_w_uq(w_uq[j])
            wuk, wvt = _layout_w_ukv(w_ukv[j])
            kc, vtc = _ctxkv(j, cache_mla_ckv, cache_kr_padded, wuk, wvt)
            for gi, grp in enumerate(groups):
                prompt = gi == 0
                outs = _abproj(grp, l, j, xs[gi], mods, w, table, win, wuq, wuk, wvt, with_cache=prompt)
                q, k, vt, rq, rk, rv, rg = outs[:7]
                if prompt:
                    oat = _attention(f"attn_p_{l}", q, k, vt, grp.n_seq, grp.seq_len)
                    ret, finals = _retention(f"ret_p_{l}", j, ld, rq, rk, rv, rg, grp.n_seq, grp.seq_len,
                                             want_final=True, finals=finals)
                    ckv_out.append(outs[7].reshape(batch, seq, KV_LORA))
                    kr_out.append(outs[8][:, MLA_NOPE:MLA_NOPE + MLA_ROPE].reshape(batch, seq, MLA_ROPE))
                else:
                    oat = _attention(f"attn_s_{l}", q, k, vt, grp.n_seq, grp.seq_len, ctx=(kc, vtc, past))
                    ret = _retention(f"ret_s_{l}", j, ld, rq, rk, rv, rg, grp.n_seq, grp.seq_len, s0=state_ret)
                xs[gi] = _merge_ffn(ffn_groups[gi], l, j, xs[gi], mods, oat, ret, w)
        else:
            for gi, grp in enumerate(groups):
                x1 = _cmix(grp, l, j, xs[gi], mods, w)
                xs[gi] = _ffn(ffn_groups[gi], l, x1, mods, w, final_norm=last)

    y_prompt = xs[0].reshape(batch, seq, d)
    y_sample = xs[1].reshape(dec_batch, dec_seq, d)
    return (y_prompt, y_sample, jnp.stack(ckv_out, axis=1), jnp.stack(kr_out, axis=1), finals)
```

```python
import functools
import math

import jax
import jax.numpy as jnp
import numpy as np
from jax import lax
from jax.experimental import pallas as pl
from jax.experimental.pallas import tpu as pltpu

F32 = jnp.float32
BF16 = jnp.bfloat16
EPS = 1e-6

MLA_HEADS = 8
MLA_NOPE = 64
MLA_ROPE = 32
MLA_V = 64
Q_LORA = 256
KV_LORA = 128
ROPE_BASE = 10000.0
GRID_W = 64
RET_HEADS = 4
RET_DK = 128
RET_DV = 128
CHUNK = 128
GM_GROUPS = 8

LANE = 128
ROW_TILE = 512
ATTN_Q_TILE = 512
KEY_CHUNK = 256
SCORE_CHUNKS = 4
SAFE_SHIFT = 50.0
BF16_SUBLANES = 16
PV_ROWS = -(-(MLA_V + 1) // BF16_SUBLANES) * BF16_SUBLANES
VMEM_LIMIT = 56 * 1024 * 1024


def _params(n_axes):
    return pltpu.CompilerParams(dimension_semantics=("arbitrary",) * n_axes,
                                vmem_limit_bytes=VMEM_LIMIT)


def _resident(shape):
    nd = len(shape)
    return pl.BlockSpec(shape, lambda *_: (0,) * nd, pipeline_mode=pl.Buffered(1))


def _slab(arr, *lead, rows=None):
    shape = arr.shape[len(lead):]
    first = 0
    if rows is not None:
        first, n = rows
        shape = (n,) + shape[1:]
    idx = tuple(lead) + (first,) + (0,) * (len(shape) - 1)
    return arr, pl.BlockSpec((None,) * len(lead) + shape, lambda *_: idx, pipeline_mode=pl.Buffered(1))


def _call(kernel_fn, operands, **kw):
    return pl.pallas_call(kernel_fn, in_specs=[s for _, s in operands], **kw)(*[a for a, _ in operands])


def _rms(x):
    return x * lax.rsqrt(jnp.mean(x * x, axis=-1, keepdims=True) + EPS)


def _dot(a, b):
    return jnp.dot(a, b, preferred_element_type=F32)


def _dot_nt(a, b):
    return lax.dot_general(a, b, (((1,), (1,)), ((), ())), preferred_element_type=F32)


def _dot_tn(a, b):
    return lax.dot_general(a, b, (((0,), (0,)), ((), ())), preferred_element_type=F32)


def _tile_heads(x, n):
    return jnp.concatenate([x] * n, axis=1)


def _rotate_half_axial(x):
    xs = x.reshape(x.shape[:-1] + (2, 2, MLA_ROPE // 4))
    return jnp.stack([-xs[..., 1, :], xs[..., 0, :]], axis=-2).reshape(x.shape)


def _ada_kernel(cond_ref, w_ref, b_ref, o_ref):
    c = cond_ref[...]
    a = (c * jax.nn.sigmoid(c)).astype(BF16)
    o_ref[0] = _dot(a, w_ref[0].astype(BF16)) + b_ref[0]


def _ada_mods(cond8, w_ada, b_ada):
    depth, d, n = w_ada.shape
    tn = n // 4
    out = pl.pallas_call(
        _ada_kernel,
        grid=(depth, n // tn),
        in_specs=[pl.BlockSpec((8, d), lambda l, j: (0, 0)),
                  pl.BlockSpec((1, d, tn), lambda l, j: (l, 0, j)),
                  pl.BlockSpec((1, 1, tn), lambda l, j: (l, 0, j))],
        out_specs=pl.BlockSpec((1, 8, tn), lambda l, j: (l, 0, j)),
        out_shape=jax.ShapeDtypeStruct((depth, 8, n), F32),
        compiler_params=_params(2),
        name="ada_mod",
    )(cond8, w_ada, b_ada.reshape(depth, 1, n))
    return out.reshape(depth, 8, 6, d)


class _Group:
    def __init__(self, name, n_seq, seq_len, cond_base, positional, tm=ROW_TILE):
        assert seq_len % tm == 0 or tm % seq_len == 0
        self.name = name
        self.n_seq, self.seq_len = n_seq, seq_len
        self.n_rows = n_seq * seq_len
        self.tm = tm
        self.n_tiles = self.n_rows // tm
        self.cond_base = cond_base
        self.tiles_per_cond = seq_len // tm if positional else self.n_tiles
        self.positional = positional

    def with_tile(self, tm):
        return _Group(self.name, self.n_seq, self.seq_len, self.cond_base, self.positional, tm)

    def cond(self, i):
        return self.cond_base + i // self.tiles_per_cond

    def table(self, i):
        return 1 + i % self.tiles_per_cond if self.positional else 0

    def rows(self, width):
        return pl.BlockSpec((self.tm, width), lambda i: (i, 0))

    def cols(self, height):
        return pl.BlockSpec((height, self.tm), lambda i: (0, i))

    def mods(self, layer, d):
        return pl.BlockSpec((1, 1, 6, d), lambda i: (layer, self.cond(i), 0, 0))


def _modulated(x, g, shift, scale):
    return (_rms(x) * g) * (1.0 + scale) + shift


FFN_ROW_TILE = 1024
FFN_SUB = 256


def _swiglu_pipeline(x_of, n_sub, mod_ref, g_ref, wg_ref, wu_ref, wd_ref, emit):
    def project(i):
        x = x_of(i)
        hb = _modulated(x, g_ref[...], mod_ref[0, 0, 3:4, :], mod_ref[0, 0, 4:5, :]).astype(BF16)
        return x, _dot(hb, wg_ref[...]), _dot(hb, wu_ref[...])

    def finish(i, x, gate, up):
        act = (gate * jax.nn.sigmoid(gate) * up).astype(BF16)
        emit(i, x + mod_ref[0, 0, 5:6, :] * _dot(act, wd_ref[...]))

    cur = project(0)
    for i in range(n_sub):
        nxt = project(i + 1) if i + 1 < n_sub else None
        finish(i, *cur)
        cur = nxt


def _sub_rows(i):
    return slice(i * FFN_SUB, (i + 1) * FFN_SUB)


def _ffn_kernel(x_ref, mod_ref, g_ref, wg_ref, wu_ref, wd_ref, fg_ref, o_ref, *, final_norm):
    def emit(i, y):
        if final_norm:
            y = _rms(y) * fg_ref[...]
        o_ref[_sub_rows(i), :] = y

    _swiglu_pipeline(lambda i: x_ref[_sub_rows(i), :], x_ref.shape[0] // FFN_SUB,
                     mod_ref, g_ref, wg_ref, wu_ref, wd_ref, emit)


def _ffn(grp, layer, x, mods, w, final_norm):
    t, d = x.shape
    return _call(
        functools.partial(_ffn_kernel, final_norm=final_norm),
        [(x, grp.rows(d)), (mods, grp.mods(layer, d)), _slab(w["norm_g"], layer, 1),
         _slab(w["ffn_gate"], layer), _slab(w["ffn_up"], layer), _slab(w["ffn_down"], layer), _slab(w["final_g"])],
        grid=(grp.n_tiles,),
        out_specs=grp.rows(d),
        out_shape=jax.ShapeDtypeStruct((t, d), F32),
        compiler_params=_params(1),
        name=f"ffn_{grp.name}_{layer}",
    )


CMIX_SUB = 256


def _cmix_kernel(x_ref, mod_ref, g_ref, win_ref, lng_ref, lnb_ref, ws_ref, bs_ref, wout_ref, o_ref, gated_ref):
    tm = x_ref.shape[0]
    w = lng_ref.shape[1]
    gw = w // GM_GROUPS

    def project(i):
        x = x_ref[i * CMIX_SUB:(i + 1) * CMIX_SUB, :]
        hb = _modulated(x, g_ref[...], mod_ref[0, 0, 0:1, :], mod_ref[0, 0, 1:2, :]).astype(BF16)
        return _dot(hb, win_ref[...])

    def gate(i, uv):
        uv = jax.nn.gelu(uv)
        u = uv[:, :w]
        v = uv[:, w:]
        mu = jnp.mean(v, axis=-1, keepdims=True)
        vc = v - mu
        var = jnp.mean(vc * vc, axis=-1, keepdims=True)
        vb = (vc * lax.rsqrt(var + EPS) * lng_ref[...] + lnb_ref[...]).astype(BF16)
        for r in range(CMIX_SUB // CHUNK):
            rs = slice(r * CHUNK, (r + 1) * CHUNK)
            out_rows = slice(i * CMIX_SUB + r * CHUNK, i * CMIX_SUB + (r + 1) * CHUNK)
            for gi in range(GM_GROUPS):
                cs = slice(gi * gw, (gi + 1) * gw)
                mixed = _dot(ws_ref[gi], vb[rs, cs]) + bs_ref[gi]
                gated_ref[out_rows, cs] = (u[rs, cs] * mixed).astype(BF16)

    def project_out(i):
        rows = slice(i * CMIX_SUB, (i + 1) * CMIX_SUB)
        o_ref[rows, :] = x_ref[rows, :] + mod_ref[0, 0, 2:3, :] * _dot(gated_ref[rows, :], wout_ref[...])

    n_sub = tm // CMIX_SUB
    uv = project(0)
    for i in range(n_sub):
        uv_next = project(i + 1) if i + 1 < n_sub else None
        gate(i, uv)
        project_out(i)
        uv = uv_next


def _cmix(grp, layer, j, x, mods, w):
    t, d = x.shape
    width = w["out_c"].shape[1]
    return _call(
        _cmix_kernel,
        [(x, grp.rows(d)), (mods, grp.mods(layer, d)), _slab(w["norm_g"], layer, 0), _slab(w["in_c"], j),
         _slab(w["ln_g_c"], j), _slab(w["ln_b_c"], j), _slab(w["s_c"], j), _slab(w["bs_c"], j), _slab(w["out_c"], j)],
        grid=(grp.n_tiles,),
        out_specs=grp.rows(d),
        out_shape=jax.ShapeDtypeStruct((t, d), F32),
        scratch_shapes=[pltpu.VMEM((grp.tm, width), BF16)],
        compiler_params=_params(1),
        name=f"cmix_{grp.name}_{layer}",
    )


_HW = MLA_HEADS * LANE
_RW = RET_HEADS * RET_DK
_C_CQ = 0
_C_CKV = _C_CQ + Q_LORA
_C_RQ = _C_CKV + KV_LORA
_C_RK = _C_RQ + _RW
_C_RV = _C_RK + _RW
_C_RG = _C_RV + _RW
_C_KRA = _C_RG + _RW
_C_KRB = _C_KRA + LANE
_C_END = _C_KRB + LANE

_Q_SCALE = (MLA_NOPE + MLA_ROPE) ** -0.5 * math.log2(math.e)


def _value_rows_t(ckv_b, wvt_ref):
    vt = _dot_nt(wvt_ref[...], ckv_b)
    row = lax.broadcasted_iota(jnp.int32, vt.shape, 0)
    return jnp.where((row & (LANE - 1)) == MLA_V, 1.0, vt)


def _abproj_kernel(*refs, with_cache):
    (x_ref, mod_ref, g_ref, tab_ref, win_ref, qg_ref, wuq_ref, kvg_ref, wuk_ref, wvt_ref,
     q_ref, k_ref, vt_ref, rq_ref, rk_ref, rv_ref, rg_ref) = refs[:17]
    x = x_ref[...]
    hb = _modulated(x, g_ref[...], mod_ref[0, 0, 0:1, :], mod_ref[0, 0, 1:2, :]).astype(BF16)
    p = _dot(hb, win_ref[...])
    rq_ref[...] = p[:, _C_RQ:_C_RK].astype(BF16)
    rk_ref[...] = (p[:, _C_RK:_C_RV] * (RET_DK ** -0.5)).astype(BF16)
    rv_ref[...] = p[:, _C_RV:_C_RG].astype(BF16)
    rg_ref[...] = p[:, _C_RG:_C_KRA]
    kr_a = p[:, _C_KRA:_C_KRB]
    kr_b = p[:, _C_KRB:_C_END]

    tab = tab_ref[...]
    cos_q, sin_q = tab[:, 0:LANE], tab[:, LANE:2 * LANE]
    cos_k, sin_k = tab[:, 2 * LANE:3 * LANE], tab[:, 3 * LANE:4 * LANE]

    cqn = (_rms(p[:, _C_CQ:_C_CKV]) * qg_ref[...]).astype(BF16)
    q2 = _dot(cqn, wuq_ref[...])
    q = q2[:, :_HW] * _tile_heads(cos_q, MLA_HEADS) + q2[:, _HW:] * _tile_heads(sin_q, MLA_HEADS)
    q_ref[...] = (q * _Q_SCALE).astype(BF16)

    ckv = _rms(p[:, _C_CKV:_C_RQ]) * kvg_ref[...]
    ckv_b = ckv.astype(BF16)
    k_rope = kr_a * cos_k + kr_b * sin_k
    k_ref[...] = (_dot(ckv_b, wuk_ref[...]) + _tile_heads(k_rope, MLA_HEADS)).astype(BF16)
    vt_ref[...] = _value_rows_t(ckv_b, wvt_ref).astype(BF16)
    if with_cache:
        ckv_ref, kr_ref = refs[17:]
        ckv_ref[...] = ckv
        kr_ref[...] = kr_a


def _abproj(grp, layer, j, x, mods, w, table, win, wuq, wuk, wvt, with_cache):
    t, d = x.shape
    row_outs = [(_HW, BF16), (_HW, BF16), None, (_RW, BF16), (_RW, BF16), (_RW, BF16), (_RW, F32)]
    if with_cache:
        row_outs += [(KV_LORA, F32), (LANE, F32)]
    out_specs = [grp.cols(_HW) if o is None else grp.rows(o[0]) for o in row_outs]
    out_shape = [jax.ShapeDtypeStruct((_HW, t), BF16) if o is None else jax.ShapeDtypeStruct((t, o[0]), o[1])
                 for o in row_outs]
    return _call(
        functools.partial(_abproj_kernel, with_cache=with_cache),
        [(x, grp.rows(d)), (mods, grp.mods(layer, d)), _slab(w["norm_g"], layer, 0),
         (table, pl.BlockSpec((grp.tm, 4 * LANE), lambda i: (grp.table(i), 0))),
         _slab(win), _slab(w["q_norm_g"], j), _slab(wuq), _slab(w["kv_norm_g"], j), _slab(wuk), _slab(wvt)],
        grid=(grp.n_tiles,),
        out_specs=out_specs,
        out_shape=out_shape,
        compiler_params=_params(1),
        name=f"abproj_{grp.name}_{layer}",
    )


def _ctxkv_kernel(ckv_ref, kr_ref, wuk_ref, wvt_ref, k_ref, vt_ref):
    ckv_b = ckv_ref[...].astype(BF16)
    k_ref[...] = (_dot(ckv_b, wuk_ref[...]) + _tile_heads(kr_ref[...], MLA_HEADS)).astype(BF16)
    vt_ref[...] = _value_rows_t(ckv_b, wvt_ref).astype(BF16)


def _ctxkv(layer_idx, cache_ckv, cache_kr_padded, wuk, wvt):
    b, _, past, kvl = cache_ckv.shape
    return pl.pallas_call(
        _ctxkv_kernel,
        grid=(b,),
        in_specs=[pl.BlockSpec((None, None, past, kvl), lambda i: (i, layer_idx, 0, 0)),
                  pl.BlockSpec((None, None, past, LANE), lambda i: (i, layer_idx, 0, 0)),
                  _resident(wuk.shape), _resident(wvt.shape)],
        out_specs=[pl.BlockSpec((past, _HW), lambda i: (i, 0)), pl.BlockSpec((_HW, past), lambda i: (0, i))],
        out_shape=[jax.ShapeDtypeStruct((b * past, _HW), BF16), jax.ShapeDtypeStruct((_HW, b * past), BF16)],
        compiler_params=_params(1),
        name=f"ctxkv_{layer_idx}",
    )(cache_ckv, cache_kr_padded, wuk, wvt)


def _attn_kernel(*refs, n_main, n_ctx):
    if n_ctx:
        q_ref, k_ref, vt_ref, kc_ref, vtc_ref, o_ref, p_scr, shift_scr, knorm_scr = refs
    else:
        q_ref, k_ref, vt_ref, o_ref, p_scr, shift_scr, knorm_scr = refs
        kc_ref = vtc_ref = None
    tq = q_ref.shape[0]
    kc = KEY_CHUNK
    chunks = [(kc_ref, vtc_ref, j) for j in range(n_ctx)] + [(k_ref, vt_ref, j) for j in range(n_main)]
    nc = len(chunks)
    score_starts = list(range(0, n_ctx, SCORE_CHUNKS)) + list(range(n_ctx, nc, SCORE_CHUNKS))
    key_refs = [r for r in (kc_ref, k_ref) if r is not None]

    def head_lanes(h):
        return slice(h * LANE, (h + 1) * LANE)

    def score_block(h, c0):
        kref, _, j = chunks[c0]
        n = min(SCORE_CHUNKS, (n_ctx if c0 < n_ctx else nc) - c0)
        return n, _dot_nt(kref[j * kc:(j + n) * kc, head_lanes(h)], q_ref[:, head_lanes(h)])

    @pl.when(pl.program_id(1) == 0)
    def _():
        for h in range(MLA_HEADS):
            best = None
            for kref in key_refs:
                kf = kref[:, head_lanes(h)].astype(F32)
                n2 = jnp.max(jnp.sum(kf * kf, axis=1, keepdims=True), axis=0, keepdims=True)
                best = n2 if best is None else jnp.maximum(best, n2)
            knorm_scr[h:h + 1, :] = jnp.broadcast_to(best, (1, LANE))

    ones = jnp.ones((8, LANE), BF16)
    worst = None
    for h in range(MLA_HEADS):
        qf = q_ref[:, head_lanes(h)].astype(F32)
        qn2 = _dot_nt(ones, (qf * qf).astype(BF16))[0:1, :]
        bound = jnp.sqrt(qn2 * knorm_scr[h:h + 1, 0:1]) * 1.01 + 1e-3
        shift_scr[h:h + 1, :] = bound
        w = jnp.max(bound, axis=1, keepdims=True)
        worst = w if worst is None else jnp.maximum(worst, w)

    @pl.when(worst[0, 0] > SAFE_SHIFT)
    def _():
        for h in range(MLA_HEADS):
            mpart = None
            for c0 in score_starts:
                n, s = score_block(h, c0)
                part = jnp.max(s.reshape(n * kc // 8, 8, tq), axis=0)
                mpart = part if mpart is None else jnp.maximum(mpart, part)
            shift_scr[h:h + 1, :] = jnp.max(mpart, axis=0, keepdims=True)

    def score_exp_block(h, c0):
        n, s = score_block(h, c0)
        p_scr[h % 2, c0 * kc:(c0 + n) * kc, :] = jnp.exp2(s - shift_scr[h:h + 1, :]).astype(BF16)

    def value_chunk(h, c, acc):
        _, vref, j = chunks[c]
        d = _dot(vref[h * LANE:h * LANE + PV_ROWS, j * kc:(j + 1) * kc], p_scr[h % 2, c * kc:(c + 1) * kc, :])
        return d if acc is None else acc + d

    for h in range(MLA_HEADS + 1):
        acc = None
        for c in range(nc):
            if h < MLA_HEADS and c in score_starts:
                score_exp_block(h, c)
            if h >= 1:
                acc = value_chunk(h - 1, c, acc)
        if h >= 1:
            inv = 1.0 / acc[MLA_V:MLA_V + 1, :]
            o_ref[(h - 1) * MLA_V:h * MLA_V, :] = (acc[:MLA_V, :] * inv).astype(o_ref.dtype)


def _attention(name, q, k, vt, n_seq, seq_len, ctx=None):
    tq = min(ATTN_Q_TILE, seq_len)
    qt = seq_len // tq
    ow = MLA_HEADS * MLA_V
    mode = dict(pipeline_mode=pl.Buffered(1)) if qt > 1 else {}
    in_specs = [pl.BlockSpec((tq, _HW), lambda b, i: (b * qt + i, 0)),
                pl.BlockSpec((seq_len, _HW), lambda b, i: (b, 0), **mode),
                pl.BlockSpec((_HW, seq_len), lambda b, i: (0, b), **mode)]
    args = [q, k, vt]
    past = 0
    if ctx is not None:
        kc, vtc, past = ctx
        in_specs += [pl.BlockSpec((past, _HW), lambda b, i: (b, 0), **mode),
                     pl.BlockSpec((_HW, past), lambda b, i: (0, b), **mode)]
        args += [kc, vtc]
    assert seq_len % KEY_CHUNK == 0 and past % KEY_CHUNK == 0
    return pl.pallas_call(
        functools.partial(_attn_kernel, n_main=seq_len // KEY_CHUNK, n_ctx=past // KEY_CHUNK),
        grid=(n_seq, qt),
        in_specs=in_specs,
        out_specs=pl.BlockSpec((ow, tq), lambda b, i: (0, b * qt + i)),
        out_shape=jax.ShapeDtypeStruct((ow, n_seq * seq_len), BF16),
        scratch_shapes=[pltpu.VMEM((2, seq_len + past, tq), BF16), pltpu.VMEM((MLA_HEADS, tq), F32),
                        pltpu.VMEM((MLA_HEADS, LANE), F32)],
        compiler_params=_params(2),
        name=name,
    )(*args)


_RET_PAIR = 2
_RET_PW = _RET_PAIR * RET_DK
RET_UNROLL = 8
RET_BODY_UNITS = 16


def _ret_kernel(*refs, n_local, n_chunks, unroll, has_s0, has_prev, want_final, layer_slot):
    refs = list(refs)
    ld_ref, q_ref, k_ref, v_ref, g_ref = refs[:5]
    refs = refs[5:]
    s0_ref = refs.pop(0) if has_s0 else None
    if has_prev:
        refs.pop(0)
    o_ref = refs.pop(0)
    sfin_ref = refs.pop(0) if want_final else None
    cross_ref, sf_ref, sb_ref = refs
    seq_len = n_chunks * CHUNK

    ii = lax.broadcasted_iota(jnp.int32, (CHUNK, CHUNK), 0).astype(F32)
    jj = lax.broadcasted_iota(jnp.int32, (CHUNK, CHUNK), 1).astype(F32)
    rel = ii - jj
    consts = []
    for h in range(_RET_PAIR):
        lg_f = -jnp.exp(ld_ref[0, h])
        lg_b = -jnp.exp(ld_ref[1, h])
        mask = (jnp.where(rel >= 0, jnp.exp(lg_f * jnp.maximum(rel, 0.0)), 0.0)
                + jnp.where(rel <= 0, jnp.exp(lg_b * jnp.maximum(-rel, 0.0)), 0.0))
        consts.append(dict(
            mask=mask,
            qd_f=jnp.exp(lg_f * (ii + 1.0)), kd_f=jnp.exp(lg_f * (CHUNK - 1.0 - ii)), cd_f=jnp.exp(lg_f * CHUNK),
            qd_b=jnp.exp(lg_b * (CHUNK - ii)), kd_b=jnp.exp(lg_b * ii), cd_b=jnp.exp(lg_b * CHUNK)))
    for s in range(n_local):
        for h in range(_RET_PAIR):
            if has_s0:
                sf_ref[s, h] = s0_ref[s, 0, h]
                sb_ref[s, h] = s0_ref[s, 1, h]
            else:
                sf_ref[s, h] = jnp.zeros((RET_DK, RET_DV), F32)
                sb_ref[s, h] = jnp.zeros((RET_DK, RET_DV), F32)

    units = [(s, h, u) for s in range(n_local) for h in range(_RET_PAIR) for u in range(unroll)]

    def rows_of(s, n):
        start = s * seq_len + n * CHUNK
        if isinstance(start, int):
            return slice(start, start + CHUNK)
        return pl.ds(pl.multiple_of(start, CHUNK), CHUNK)

    def head_lanes(h):
        return slice(h * RET_DK, (h + 1) * RET_DK)

    def bwd_block(t):
        loaded, kv = {}, {}
        for (s, h, u) in units:
            rows, hs = rows_of(s, n_chunks - 1 - (t * unroll + u)), head_lanes(h)
            loaded[s, h, u] = (rows, hs, q_ref[rows, hs])
            kv[s, h, u] = _dot_tn((k_ref[rows, hs].astype(F32) * consts[h]["kd_b"]).astype(BF16), v_ref[rows, hs])
        for s in range(n_local):
            for h in range(_RET_PAIR):
                c = consts[h]
                state = sb_ref[s, h]
                for u in range(unroll):
                    rows, hs, q = loaded[s, h, u]
                    cross_ref[rows, hs] = _dot(q, state.astype(BF16)) * c["qd_b"]
                    state = state * c["cd_b"] + kv[s, h, u]
                sb_ref[s, h] = state

    def fwd_block(t):
        loaded = {}
        for (s, h, u) in units:
            rows, hs = rows_of(s, t * unroll + u), head_lanes(h)
            loaded[s, h, u] = (rows, hs, q_ref[rows, hs], k_ref[rows, hs], v_ref[rows, hs])
        qk, kv = {}, {}
        for key in units:
            rows, hs, q, k, v = loaded[key]
            qk[key] = _dot_nt(q, k)
            kv[key] = _dot_tn((k.astype(F32) * consts[key[1]]["kd_f"]).astype(BF16), v)
        inner, cross = {}, {}
        for s in range(n_local):
            for h in range(_RET_PAIR):
                c = consts[h]
                state = sf_ref[s, h]
                for u in range(unroll):
                    rows, hs, q, k, v = loaded[s, h, u]
                    inner[s, h, u] = _dot((qk[s, h, u] * c["mask"]).astype(BF16), v)
                    cross[s, h, u] = _dot(q, state.astype(BF16))
                    state = state * c["cd_f"] + kv[s, h, u]
                sf_ref[s, h] = state
        for key in units:
            rows, hs = loaded[key][:2]
            o = inner[key] + cross[key] * consts[key[1]]["qd_f"] + cross_ref[rows, hs]
            gate = g_ref[rows, hs]
            o_ref[rows, hs] = (_rms(o) * (gate * jax.nn.sigmoid(gate))).astype(o_ref.dtype)

    n_blocks = n_chunks // unroll
    if n_blocks == 1:
        bwd_block(0)
        fwd_block(0)
    else:
        def loop_body(block):
            def body(t, carry):
                block(t)
                return carry
            return body
        lax.fori_loop(0, n_blocks, loop_body(bwd_block), 0)
        lax.fori_loop(0, n_blocks, loop_body(fwd_block), 0)
    if want_final:
        if has_prev:
            slot = sfin_ref
        else:
            for other in range(sfin_ref.shape[1]):
                if other != layer_slot:
                    sfin_ref[:, other] = jnp.zeros(sfin_ref.shape[:1] + sfin_ref.shape[2:], F32)
            slot = sfin_ref.at[:, layer_slot]
        for s in range(n_local):
            for h in range(_RET_PAIR):
                slot[s, 0, h] = sf_ref[s, h]
                slot[s, 1, h] = sb_ref[s, h]


def _retention(name, j, ld, rq, rk, rv, rg, n_seq, seq_len, s0=None, want_final=False, finals=None):
    pairs = RET_HEADS // _RET_PAIR
    n_chunks = seq_len // CHUNK
    n_layers = ld.shape[0]
    unroll = min(RET_UNROLL, n_chunks)
    n_local = max(1, RET_BODY_UNITS // n_chunks)
    assert n_chunks % unroll == 0 and n_seq % n_local == 0
    seq = pl.BlockSpec((n_local * seq_len, _RET_PW), lambda b, p: (b, p))
    state = pl.BlockSpec((n_local, None, 2, _RET_PAIR, RET_DK, RET_DV), lambda b, p: (b, j, 0, p, 0, 0))
    operands = [(ld, pl.BlockSpec((None, 2, _RET_PAIR, CHUNK, CHUNK), lambda b, p: (j, 0, p, 0, 0))),
                (rq, seq), (rk, seq), (rv, seq), (rg, seq)]
    if s0 is not None:
        operands.append((s0, state))
    aliases = {}
    if finals is not None:
        aliases = {len(operands): 1}
        operands.append((finals, pl.BlockSpec(memory_space=pl.ANY)))
    out_specs = [seq]
    out_shape = [jax.ShapeDtypeStruct((n_seq * seq_len, _RW), BF16)]
    if want_final:
        all_slots = pl.BlockSpec((n_local, n_layers, 2, _RET_PAIR, RET_DK, RET_DV), lambda b, p: (b, 0, 0, p, 0, 0))
        out_specs.append(state if finals is not None else all_slots)
        out_shape.append(jax.ShapeDtypeStruct((n_seq, n_layers, 2, RET_HEADS, RET_DK, RET_DV), F32))
    res = _call(
        functools.partial(_ret_kernel, n_local=n_local, n_chunks=n_chunks, unroll=unroll, has_s0=s0 is not None,
                          has_prev=finals is not None, want_final=want_final, layer_slot=j),
        operands,
        grid=(n_seq // n_local, pairs),
        out_specs=out_specs,
        out_shape=out_shape,
        input_output_aliases=aliases,
        scratch_shapes=[pltpu.VMEM((n_local * seq_len, _RET_PW), F32),
                        pltpu.VMEM((n_local, _RET_PAIR, RET_DK, RET_DV), F32),
                        pltpu.VMEM((n_local, _RET_PAIR, RET_DK, RET_DV), F32)],
        compiler_params=_params(2),
        name=name,
    )
    return res if want_final else res[0]


def _merge_ffn_kernel(x_ref, mod_ref, oat_ref, ret_ref, woa_ref, wor_ref, g_ref, wg_ref, wu_ref, wd_ref, o_ref):
    n_sub = x_ref.shape[0] // FFN_SUB
    mixed = []
    for i in range(n_sub):
        rows = _sub_rows(i)
        y = _dot_tn(oat_ref[:, rows], woa_ref[...]) + _dot(ret_ref[rows, :], wor_ref[...])
        mixed.append(x_ref[rows, :] + mod_ref[0, 0, 2:3, :] * y)

    def emit(i, y):
        o_ref[_sub_rows(i), :] = y

    _swiglu_pipeline(lambda i: mixed[i], n_sub, mod_ref, g_ref, wg_ref, wu_ref, wd_ref, emit)


def _merge_ffn(grp, layer, j, x, mods, o_attn_t, ret, w):
    t, d = x.shape
    aw = o_attn_t.shape[0]
    assert aw == ret.shape[1]
    return _call(
        _merge_ffn_kernel,
        [(x, grp.rows(d)), (mods, grp.mods(layer, d)), (o_attn_t, grp.cols(aw)), (ret, grp.rows(aw)),
         _slab(w["o_ab"], j, rows=(0, aw)), _slab(w["o_ab"], j, rows=(1, aw)), _slab(w["norm_g"], layer, 1),
         _slab(w["ffn_gate"], layer), _slab(w["ffn_up"], layer), _slab(w["ffn_down"], layer)],
        grid=(grp.n_tiles,),
        out_specs=grp.rows(d),
        out_shape=jax.ShapeDtypeStruct((t, d), F32),
        compiler_params=_params(1),
        name=f"merge_ffn_{grp.name}_{layer}",
    )


def _rope_table(dec_seq, tm):
    f32 = np.float32
    rows = dec_seq // GRID_W
    row = np.repeat(np.arange(rows), GRID_W).astype(f32)
    col = np.tile(np.arange(GRID_W), rows).astype(f32)
    half = MLA_ROPE // 2
    inv = (1.0 / np.power(f32(ROPE_BASE), np.arange(0, half, 2, dtype=f32) / f32(half))).astype(f32)
    ang = np.stack([row[:, None] * inv, col[:, None] * inv], axis=1)
    ang = np.stack([ang, ang], axis=2).reshape(dec_seq, MLA_ROPE)
    cos, sin = np.cos(ang).astype(f32), np.sin(ang).astype(f32)
    lo = MLA_NOPE + MLA_ROPE
    ones, zeros = np.ones((dec_seq, lo), f32), np.zeros((dec_seq, lo), f32)
    pos = np.concatenate([ones, cos, zeros, sin, zeros, cos, zeros, sin], axis=1)
    flat_k = np.concatenate([np.zeros((tm, MLA_NOPE), f32), np.ones((tm, MLA_ROPE), f32),
                             np.zeros((tm, MLA_ROPE), f32)], axis=1)
    flat = np.concatenate([np.ones((tm, LANE), f32), np.zeros((tm, LANE), f32), flat_k,
                           np.zeros((tm, LANE), f32)], axis=1)
    return jnp.asarray(np.concatenate([flat, pos], axis=0))


def _layout_w_in(w):
    w = w.astype(BF16)
    d = w.shape[0]
    o = 0
    parts = {}
    for name, width in (("cq", Q_LORA), ("ckv", KV_LORA), ("kr", MLA_ROPE), ("rq", _RW), ("rk", _RW),
                        ("rv", _RW), ("rg", _RW)):
        parts[name] = w[:, o:o + width]
        o += width
    kr = parts["kr"]
    kr_a = jnp.concatenate([jnp.zeros((d, MLA_NOPE), w.dtype), kr, kr], axis=1)
    kr_b = jnp.concatenate([jnp.zeros((d, MLA_NOPE + MLA_ROPE), w.dtype), _rotate_half_axial(kr)], axis=1)
    return jnp.concatenate([parts["cq"], parts["ckv"], parts["rq"], parts["rk"], parts["rv"], parts["rg"],
                            kr_a, kr_b], axis=1).astype(BF16)


def _layout_w_uq(w):
    w = w.astype(BF16)
    r = w.shape[0]
    wh = w.reshape(r, MLA_HEADS, MLA_NOPE + MLA_ROPE)
    nope, rope = wh[..., :MLA_NOPE], wh[..., MLA_NOPE:]
    main = jnp.concatenate([nope, rope, rope], axis=-1).reshape(r, _HW)
    rot = jnp.concatenate([jnp.zeros(nope.shape[:2] + (MLA_NOPE + MLA_ROPE,), w.dtype),
                           _rotate_half_axial(rope)], axis=-1).reshape(r, _HW)
    return jnp.concatenate([main, rot], axis=1).astype(BF16)


def _layout_w_ukv(w):
    w = w.astype(BF16)
    r = w.shape[0]
    wh = w.reshape(r, MLA_HEADS, MLA_NOPE + MLA_V)
    kn, v = wh[..., :MLA_NOPE], wh[..., MLA_NOPE:]
    wuk = jnp.concatenate([kn, jnp.zeros((r, MLA_HEADS, LANE - MLA_NOPE), w.dtype)], axis=-1).reshape(r, _HW)
    wv = jnp.concatenate([v, jnp.zeros((r, MLA_HEADS, LANE - MLA_V), w.dtype)], axis=-1).reshape(r, _HW)
    return wuk.astype(BF16), wv.T.astype(BF16)


def kernel(x_prompt, x_sample, cache_mla_ckv, cache_mla_krope, state_ret, c, c_ctx, w_ada, b_ada, norm_g, w_in_ab, q_norm_g, w_uq, kv_norm_g, w_ukv, ret_log_decay, w_o_ab, w_in_c, ln_g_c, ln_b_c, w_s_c, b_s_c, w_out_c, w_ffn_gate, w_ffn_up, w_ffn_down, final_norm_g):
    batch, seq, d = x_prompt.shape
    dec_batch, dec_seq, _ = x_sample.shape
    depth = w_ada.shape[0]
    assert depth % 2 == 0
    past = cache_mla_ckv.shape[2]
    groups = [_Group("p", batch, seq, 0, positional=False), _Group("s", dec_batch, dec_seq, 1, positional=True)]
    ffn_groups = [grp.with_tile(FFN_ROW_TILE) for grp in groups]
    xs = [x_prompt.reshape(batch * seq, d), x_sample.reshape(dec_batch * dec_seq, d)]

    cond8 = jnp.zeros((8, d), F32).at[0].set(c_ctx).at[1:1 + dec_batch].set(c)
    mods = _ada_mods(cond8, w_ada, b_ada)

    table = _rope_table(dec_seq, ROW_TILE)
    cache_kr_padded = jnp.pad(cache_mla_krope, ((0, 0), (0, 0), (0, 0), (MLA_NOPE, LANE - MLA_NOPE - MLA_ROPE)))
    n_ab = w_in_ab.shape[0]
    gw = w_out_c.shape[1] // GM_GROUPS
    w = dict(
        norm_g=norm_g.reshape(depth, 2, 1, d), final_g=final_norm_g.reshape(1, d),
        ffn_gate=w_ffn_gate.astype(BF16), ffn_up=w_ffn_up.astype(BF16), ffn_down=w_ffn_down.astype(BF16),
        q_norm_g=q_norm_g.reshape(n_ab, 1, -1), kv_norm_g=kv_norm_g.reshape(n_ab, 1, -1), o_ab=w_o_ab.astype(BF16),
        in_c=w_in_c.astype(BF16), out_c=w_out_c.astype(BF16), s_c=w_s_c.astype(BF16),
        ln_g_c=ln_g_c[:, None, :], ln_b_c=ln_b_c[:, None, :],
        bs_c=jnp.broadcast_to(b_s_c[:, :, :, None], b_s_c.shape + (gw,)))
    ld = jnp.broadcast_to(ret_log_decay[:, :, :, None, None], ret_log_decay.shape + (CHUNK, CHUNK))

    ckv_out, kr_out, finals = [], [], None
    for l in range(depth):
        j = l // 2
        last = l == depth - 1
        if l % 2 == 0:
            win, wuq = _layout_w_in(w_in_ab[j]), _layout_w_uq(w_uq[j])
            wuk, wvt = _layout_w_ukv(w_ukv[j])
            kc, vtc = _ctxkv(j, cache_mla_ckv, cache_kr_padded, wuk, wvt)
            for gi, grp in enumerate(groups):
                prompt = gi == 0
                outs = _abproj(grp, l, j, xs[gi], mods, w, table, win, wuq, wuk, wvt, with_cache=prompt)
                q, k, vt, rq, rk, rv, rg = outs[:7]
                if prompt:
                    oat = _attention(f"attn_p_{l}", q, k, vt, grp.n_seq, grp.seq_len)
                    ret, finals = _retention(f"ret_p_{l}", j, ld, rq, rk, rv, rg, grp.n_seq, grp.seq_len,
                                             want_final=True, finals=finals)
                    ckv_out.append(outs[7].reshape(batch, seq, KV_LORA))
                    kr_out.append(outs[8][:, MLA_NOPE:MLA_NOPE + MLA_ROPE].reshape(batch, seq, MLA_ROPE))
                else:
                    oat = _attention(f"attn_s_{l}", q, k, vt, grp.n_seq, grp.seq_len, ctx=(kc, vtc, past))
                    ret = _retention(f"ret_s_{l}", j, ld, rq, rk, rv, rg, grp.n_seq, grp.seq_len, s0=state_ret)
                xs[gi] = _merge_ffn(ffn_groups[gi], l, j, xs[gi], mods, oat, ret, w)
        else:
            for gi, grp in enumerate(groups):
                x1 = _cmix(grp, l, j, xs[gi], mods, w)
                xs[gi] = _ffn(ffn_groups[gi], l, x1, mods, w, final_norm=last)

    y_prompt = xs[0].reshape(batch, seq, d)
    y_sample = xs[1].reshape(dec_batch, dec_seq, d)
    return (y_prompt, y_sample, jnp.stack(ckv_out, axis=1), jnp.stack(kr_out, axis=1), finals)
```

```python
import functools
import math

import jax
import jax.numpy as jnp
import numpy as np
from jax import lax
from jax.experimental import pallas as pl
from jax.experimental.pallas import tpu as pltpu

F32 = jnp.float32
BF16 = jnp.bfloat16
EPS = 1e-6

MLA_HEADS = 8
MLA_NOPE = 64
MLA_ROPE = 32
MLA_V = 64
Q_LORA = 256
KV_LORA = 128
ROPE_BASE = 10000.0
GRID_W = 64
RET_HEADS = 4
RET_DK = 128
RET_DV = 128
CHUNK = 128
GM_GROUPS = 8

LANE = 128
ROW_TILE = 512
ATTN_Q_TILE = 512
KEY_CHUNK = 256
SCORE_CHUNKS = 4
SAFE_SHIFT = 50.0
BF16_SUBLANES = 16
PV_ROWS = -(-(MLA_V + 1) // BF16_SUBLANES) * BF16_SUBLANES
VMEM_LIMIT = 56 * 1024 * 1024


def _params(n_axes):
    return pltpu.CompilerParams(dimension_semantics=("arbitrary",) * n_axes,
                                vmem_limit_bytes=VMEM_LIMIT)


def _resident(shape):
    nd = len(shape)
    return pl.BlockSpec(shape, lambda *_: (0,) * nd, pipeline_mode=pl.Buffered(1))


def _slab(arr, *lead, rows=None):
    shape = arr.shape[len(lead):]
    first = 0
    if rows is not None:
        first, n = rows
        shape = (n,) + shape[1:]
    idx = tuple(lead) + (first,) + (0,) * (len(shape) - 1)
    return arr, pl.BlockSpec((None,) * len(lead) + shape, lambda *_: idx, pipeline_mode=pl.Buffered(1))


def _call(kernel_fn, operands, **kw):
    return pl.pallas_call(kernel_fn, in_specs=[s for _, s in operands], **kw)(*[a for a, _ in operands])


def _rms(x):
    return x * lax.rsqrt(jnp.mean(x * x, axis=-1, keepdims=True) + EPS)


def _dot(a, b):
    return jnp.dot(a, b, preferred_element_type=F32)


def _dot_nt(a, b):
    return lax.dot_general(a, b, (((1,), (1,)), ((), ())), preferred_element_type=F32)


def _dot_tn(a, b):
    return lax.dot_general(a, b, (((0,), (0,)), ((), ())), preferred_element_type=F32)


def _tile_heads(x, n):
    return jnp.concatenate([x] * n, axis=1)


def _rotate_half_axial(x):
    xs = x.reshape(x.shape[:-1] + (2, 2, MLA_ROPE // 4))
    return jnp.stack([-xs[..., 1, :], xs[..., 0, :]], axis=-2).reshape(x.shape)


def _ada_kernel(cond_ref, w_ref, b_ref, o_ref):
    c = cond_ref[...]
    a = (c * jax.nn.sigmoid(c)).astype(BF16)
    o_ref[0] = _dot(a, w_ref[0].astype(BF16)) + b_ref[0]


def _ada_mods(cond8, w_ada, b_ada):
    depth, d, n = w_ada.shape
    tn = n // 4
    out = pl.pallas_call(
        _ada_kernel,
        grid=(depth, n // tn),
        in_specs=[pl.BlockSpec((8, d), lambda l, j: (0, 0)),
                  pl.BlockSpec((1, d, tn), lambda l, j: (l, 0, j)),
                  pl.BlockSpec((1, 1, tn), lambda l, j: (l, 0, j))],
        out_specs=pl.BlockSpec((1, 8, tn), lambda l, j: (l, 0, j)),
        out_shape=jax.ShapeDtypeStruct((depth, 8, n), F32),
        compiler_params=_params(2),
        name="ada_mod",
    )(cond8, w_ada, b_ada.reshape(depth, 1, n))
    return out.reshape(depth, 8, 6, d)


class _Group:
    def __init__(self, name, n_seq, seq_len, cond_base, positional, tm=ROW_TILE):
        assert seq_len % tm == 0 or tm % seq_len == 0
        self.name = name
        self.n_seq, self.seq_len = n_seq, seq_len
        self.n_rows = n_seq * seq_len
        self.tm = tm
        self.n_tiles = self.n_rows // tm
        self.cond_base = cond_base
        self.tiles_per_cond = seq_len // tm if positional else self.n_tiles
        self.positional = positional

    def with_tile(self, tm):
        return _Group(self.name, self.n_seq, self.seq_len, self.cond_base, self.positional, tm)

    def cond(self, i):
        return self.cond_base + i // self.tiles_per_cond

    def table(self, i):
        return 1 + i % self.tiles_per_cond if self.positional else 0

    def rows(self, width):
        return pl.BlockSpec((self.tm, width), lambda i: (i, 0))

    def cols(self, height):
        return pl.BlockSpec((height, self.tm), lambda i: (0, i))

    def mods(self, layer, d):
        return pl.BlockSpec((1, 1, 6, d), lambda i: (layer, self.cond(i), 0, 0))


def _modulated(x, g, shift, scale):
    return (_rms(x) * g) * (1.0 + scale) + shift


FFN_ROW_TILE = 1024
FFN_SUB = 256


def _swiglu_pipeline(x_of, n_sub, mod_ref, g_ref, wg_ref, wu_ref, wd_ref, emit):
    def project(i):
        x = x_of(i)
        hb = _modulated(x, g_ref[...], mod_ref[0, 0, 3:4, :], mod_ref[0, 0, 4:5, :]).astype(BF16)
        return x, _dot(hb, wg_ref[...]), _dot(hb, wu_ref[...])

    def finish(i, x, gate, up):
        act = (gate * jax.nn.sigmoid(gate) * up).astype(BF16)
        emit(i, x + mod_ref[0, 0, 5:6, :] * _dot(act, wd_ref[...]))

    cur = project(0)
    for i in range(n_sub):
        nxt = project(i + 1) if i + 1 < n_sub else None
        finish(i, *cur)
        cur = nxt


def _sub_rows(i):
    return slice(i * FFN_SUB, (i + 1) * FFN_SUB)


def _ffn_kernel(x_ref, mod_ref, g_ref, wg_ref, wu_ref, wd_ref, fg_ref, o_ref, *, final_norm):
    def emit(i, y):
        if final_norm:
            y = _rms(y) * fg_ref[...]
        o_ref[_sub_rows(i), :] = y

    _swiglu_pipeline(lambda i: x_ref[_sub_rows(i), :], x_ref.shape[0] // FFN_SUB,
                     mod_ref, g_ref, wg_ref, wu_ref, wd_ref, emit)


def _ffn(grp, layer, x, mods, w, final_norm):
    t, d = x.shape
    return _call(
        functools.partial(_ffn_kernel, final_norm=final_norm),
        [(x, grp.rows(d)), (mods, grp.mods(layer, d)), _slab(w["norm_g"], layer, 1),
         _slab(w["ffn_gate"], layer), _slab(w["ffn_up"], layer), _slab(w["ffn_down"], layer), _slab(w["final_g"])],
        grid=(grp.n_tiles,),
        out_specs=grp.rows(d),
        out_shape=jax.ShapeDtypeStruct((t, d), F32),
        compiler_params=_params(1),
        name=f"ffn_{grp.name}_{layer}",
    )


CMIX_SUB = 256


def _cmix_kernel(x_ref, mod_ref, g_ref, win_ref, lng_ref, lnb_ref, ws_ref, bs_ref, wout_ref, o_ref, gated_ref):
    tm = x_ref.shape[0]
    w = lng_ref.shape[1]
    gw = w // GM_GROUPS

    def project(i):
        x = x_ref[i * CMIX_SUB:(i + 1) * CMIX_SUB, :]
        hb = _modulated(x, g_ref[...], mod_ref[0, 0, 0:1, :], mod_ref[0, 0, 1:2, :]).astype(BF16)
        return _dot(hb, win_ref[...])

    def gate(i, uv):
        uv = jax.nn.gelu(uv)
        u = uv[:, :w]
        v = uv[:, w:]
        mu = jnp.mean(v, axis=-1, keepdims=True)
        vc = v - mu
        var = jnp.mean(vc * vc, axis=-1, keepdims=True)
        vb = (vc * lax.rsqrt(var + EPS) * lng_ref[...] + lnb_ref[...]).astype(BF16)
        for r in range(CMIX_SUB // CHUNK):
            rs = slice(r * CHUNK, (r + 1) * CHUNK)
            out_rows = slice(i * CMIX_SUB + r * CHUNK, i * CMIX_SUB + (r + 1) * CHUNK)
            for gi in range(GM_GROUPS):
                cs = slice(gi * gw, (gi + 1) * gw)
                mixed = _dot(ws_ref[gi], vb[rs, cs]) + bs_ref[gi]
                gated_ref[out_rows, cs] = (u[rs, cs] * mixed).astype(BF16)

    def project_out(i):
        rows = slice(i * CMIX_SUB, (i + 1) * CMIX_SUB)
        o_ref[rows, :] = x_ref[rows, :] + mod_ref[0, 0, 2:3, :] * _dot(gated_ref[rows, :], wout_ref[...])

    n_sub = tm // CMIX_SUB
    uv = project(0)
    for i in range(n_sub):
        uv_next = project(i + 1) if i + 1 < n_sub else None
        gate(i, uv)
        project_out(i)
        uv = uv_next


def _cmix(grp, layer, j, x, mods, w):
    t, d = x.shape
    width = w["out_c"].shape[1]
    return _call(
        _cmix_kernel,
        [(x, grp.rows(d)), (mods, grp.mods(layer, d)), _slab(w["norm_g"], layer, 0), _slab(w["in_c"], j),
         _slab(w["ln_g_c"], j), _slab(w["ln_b_c"], j), _slab(w["s_c"], j), _slab(w["bs_c"], j), _slab(w["out_c"], j)],
        grid=(grp.n_tiles,),
        out_specs=grp.rows(d),
        out_shape=jax.ShapeDtypeStruct((t, d), F32),
        scratch_shapes=[pltpu.VMEM((grp.tm, width), BF16)],
        compiler_params=_params(1),
        name=f"cmix_{grp.name}_{layer}",
    )


_HW = MLA_HEADS * LANE
_RW = RET_HEADS * RET_DK
_C_CQ = 0
_C_CKV = _C_CQ + Q_LORA
_C_RQ = _C_CKV + KV_LORA
_C_RK = _C_RQ + _RW
_C_RV = _C_RK + _RW
_C_RG = _C_RV + _RW
_C_KRA = _C_RG + _RW
_C_KRB = _C_KRA + LANE
_C_END = _C_KRB + LANE

_Q_SCALE = (MLA_NOPE + MLA_ROPE) ** -0.5 * math.log2(math.e)


def _value_rows_t(ckv_b, wvt_ref):
    vt = _dot_nt(wvt_ref[...], ckv_b)
    row = lax.broadcasted_iota(jnp.int32, vt.shape, 0)
    return jnp.where((row & (LANE - 1)) == MLA_V, 1.0, vt)


def _abproj_kernel(*refs, with_cache):
    (x_ref, mod_ref, g_ref, tab_ref, win_ref, qg_ref, wuq_ref, kvg_ref, wuk_ref, wvt_ref,
     q_ref, k_ref, vt_ref, rq_ref, rk_ref, rv_ref, rg_ref) = refs[:17]
    x = x_ref[...]
    hb = _modulated(x, g_ref[...], mod_ref[0, 0, 0:1, :], mod_ref[0, 0, 1:2, :]).astype(BF16)
    p = _dot(hb, win_ref[...])
    rq_ref[...] = p[:, _C_RQ:_C_RK].astype(BF16)
    rk_ref[...] = (p[:, _C_RK:_C_RV] * (RET_DK ** -0.5)).astype(BF16)
    rv_ref[...] = p[:, _C_RV:_C_RG].astype(BF16)
    rg_ref[...] = p[:, _C_RG:_C_KRA]
    kr_a = p[:, _C_KRA:_C_KRB]
    kr_b = p[:, _C_KRB:_C_END]

    tab = tab_ref[...]
    cos_q, sin_q = tab[:, 0:LANE], tab[:, LANE:2 * LANE]
    cos_k, sin_k = tab[:, 2 * LANE:3 * LANE], tab[:, 3 * LANE:4 * LANE]

    cqn = (_rms(p[:, _C_CQ:_C_CKV]) * qg_ref[...]).astype(BF16)
    q2 = _dot(cqn, wuq_ref[...])
    q = q2[:, :_HW] * _tile_heads(cos_q, MLA_HEADS) + q2[:, _HW:] * _tile_heads(sin_q, MLA_HEADS)
    q_ref[...] = (q * _Q_SCALE).astype(BF16)

    ckv = _rms(p[:, _C_CKV:_C_RQ]) * kvg_ref[...]
    ckv_b = ckv.astype(BF16)
    k_rope = kr_a * cos_k + kr_b * sin_k
    k_ref[...] = (_dot(ckv_b, wuk_ref[...]) + _tile_heads(k_rope, MLA_HEADS)).astype(BF16)
    vt_ref[...] = _value_rows_t(ckv_b, wvt_ref).astype(BF16)
    if with_cache:
        ckv_ref, kr_ref = refs[17:]
        ckv_ref[...] = ckv
        kr_ref[...] = kr_a


def _abproj(grp, layer, j, x, mods, w, table, win, wuq, wuk, wvt, with_cache):
    t, d = x.shape
    row_outs = [(_HW, BF16), (_HW, BF16), None, (_RW, BF16), (_RW, BF16), (_RW, BF16), (_RW, F32)]
    if with_cache:
        row_outs += [(KV_LORA, F32), (LANE, F32)]
    out_specs = [grp.cols(_HW) if o is None else grp.rows(o[0]) for o in row_outs]
    out_shape = [jax.ShapeDtypeStruct((_HW, t), BF16) if o is None else jax.ShapeDtypeStruct((t, o[0]), o[1])
                 for o in row_outs]
    return _call(
        functools.partial(_abproj_kernel, with_cache=with_cache),
        [(x, grp.rows(d)), (mods, grp.mods(layer, d)), _slab(w["norm_g"], layer, 0),
         (table, pl.BlockSpec((grp.tm, 4 * LANE), lambda i: (grp.table(i), 0))),
         _slab(win), _slab(w["q_norm_g"], j), _slab(wuq), _slab(w["kv_norm_g"], j), _slab(wuk), _slab(wvt)],
        grid=(grp.n_tiles,),
        out_specs=out_specs,
        out_shape=out_shape,
        compiler_params=_params(1),
        name=f"abproj_{grp.name}_{layer}",
    )


def _ctxkv_kernel(ckv_ref, kr_ref, wuk_ref, wvt_ref, k_ref, vt_ref):
    ckv_b = ckv_ref[...].astype(BF16)
    k_ref[...] = (_dot(ckv_b, wuk_ref[...]) + _tile_heads(kr_ref[...], MLA_HEADS)).astype(BF16)
    vt_ref[...] = _value_rows_t(ckv_b, wvt_ref).astype(BF16)


def _ctxkv(layer_idx, cache_ckv, cache_kr_padded, wuk, wvt):
    b, _, past, kvl = cache_ckv.shape
    return pl.pallas_call(
        _ctxkv_kernel,
        grid=(b,),
        in_specs=[pl.BlockSpec((None, None, past, kvl), lambda i: (i, layer_idx, 0, 0)),
                  pl.BlockSpec((None, None, past, LANE), lambda i: (i, layer_idx, 0, 0)),
                  _resident(wuk.shape), _resident(wvt.shape)],
        out_specs=[pl.BlockSpec((past, _HW), lambda i: (i, 0)), pl.BlockSpec((_HW, past), lambda i: (0, i))],
        out_shape=[jax.ShapeDtypeStruct((b * past, _HW), BF16), jax.ShapeDtypeStruct((_HW, b * past), BF16)],
        compiler_params=_params(1),
        name=f"ctxkv_{layer_idx}",
    )(cache_ckv, cache_kr_padded, wuk, wvt)


def _attn_kernel(*refs, n_main, n_ctx):
    if n_ctx:
        q_ref, k_ref, vt_ref, kc_ref, vtc_ref, o_ref, p_scr, shift_scr, knorm_scr = refs
    else:
        q_ref, k_ref, vt_ref, o_ref, p_scr, shift_scr, knorm_scr = refs
        kc_ref = vtc_ref = None
    tq = q_ref.shape[0]
    kc = KEY_CHUNK
    chunks = [(kc_ref, vtc_ref, j) for j in range(n_ctx)] + [(k_ref, vt_ref, j) for j in range(n_main)]
    nc = len(chunks)
    score_starts = list(range(0, n_ctx, SCORE_CHUNKS)) + list(range(n_ctx, nc, SCORE_CHUNKS))
    key_refs = [r for r in (kc_ref, k_ref) if r is not None]

    def head_lanes(h):
        return slice(h * LANE, (h + 1) * LANE)

    def score_block(h, c0):
        kref, _, j = chunks[c0]
        n = min(SCORE_CHUNKS, (n_ctx if c0 < n_ctx else nc) - c0)
        return n, _dot_nt(kref[j * kc:(j + n) * kc, head_lanes(h)], q_ref[:, head_lanes(h)])

    @pl.when(pl.program_id(1) == 0)
    def _():
        for h in range(MLA_HEADS):
            best = None
            for kref in key_refs:
                kf = kref[:, head_lanes(h)].astype(F32)
                n2 = jnp.max(jnp.sum(kf * kf, axis=1, keepdims=True), axis=0, keepdims=True)
                best = n2 if best is None else jnp.maximum(best, n2)
            knorm_scr[h:h + 1, :] = jnp.broadcast_to(best, (1, LANE))

    ones = jnp.ones((8, LANE), BF16)
    worst = None
    for h in range(MLA_HEADS):
        qf = q_ref[:, head_lanes(h)].astype(F32)
        qn2 = _dot_nt(ones, (qf * qf).astype(BF16))[0:1, :]
        bound = jnp.sqrt(qn2 * knorm_scr[h:h + 1, 0:1]) * 1.01 + 1e-3
        shift_scr[h:h + 1, :] = bound
        w = jnp.max(bound, axis=1, keepdims=True)
        worst = w if worst is None else jnp.maximum(worst, w)

    @pl.when(worst[0, 0] > SAFE_SHIFT)
    def _():
        for h in range(MLA_HEADS):
            mpart = None
            for c0 in score_starts:
                n, s = score_block(h, c0)
                part = jnp.max(s.reshape(n * kc // 8, 8, tq), axis=0)
                mpart = part if mpart is None else jnp.maximum(mpart, part)
            shift_scr[h:h + 1, :] = jnp.max(mpart, axis=0, keepdims=True)

    def score_exp_block(h, c0):
        n, s = score_block(h, c0)
        p_scr[h % 2, c0 * kc:(c0 + n) * kc, :] = jnp.exp2(s - shift_scr[h:h + 1, :]).astype(BF16)

    def value_chunk(h, c, acc):
        _, vref, j = chunks[c]
        d = _dot(vref[h * LANE:h * LANE + PV_ROWS, j * kc:(j + 1) * kc], p_scr[h % 2, c * kc:(c + 1) * kc, :])
        return d if acc is None else acc + d

    for h in range(MLA_HEADS + 1):
        acc = None
        for c in range(nc):
            if h < MLA_HEADS and c in score_starts:
                score_exp_block(h, c)
            if h >= 1:
                acc = value_chunk(h - 1, c, acc)
        if h >= 1:
            inv = 1.0 / acc[MLA_V:MLA_V + 1, :]
            o_ref[(h - 1) * MLA_V:h * MLA_V, :] = (acc[:MLA_V, :] * inv).astype(o_ref.dtype)


def _attention(name, q, k, vt, n_seq, seq_len, ctx=None):
    tq = min(ATTN_Q_TILE, seq_len)
    qt = seq_len // tq
    ow = MLA_HEADS * MLA_V
    mode = {}
    in_specs = [pl.BlockSpec((tq, _HW), lambda b, i: (b * qt + i, 0)),
                pl.BlockSpec((seq_len, _HW), lambda b, i: (b, 0), **mode),
                pl.BlockSpec((_HW, seq_len), lambda b, i: (0, b), **mode)]
    args = [q, k, vt]
    past = 0
    if ctx is not None:
        kc, vtc, past = ctx
        in_specs += [pl.BlockSpec((past, _HW), lambda b, i: (b, 0), **mode),
                     pl.BlockSpec((_HW, past), lambda b, i: (0, b), **mode)]
        args += [kc, vtc]
    assert seq_len % KEY_CHUNK == 0 and past % KEY_CHUNK == 0
    return pl.pallas_call(
        functools.partial(_attn_kernel, n_main=seq_len // KEY_CHUNK, n_ctx=past // KEY_CHUNK),
        grid=(n_seq, qt),
        in_specs=in_specs,
        out_specs=pl.BlockSpec((ow, tq), lambda b, i: (0, b * qt + i)),
        out_shape=jax.ShapeDtypeStruct((ow, n_seq * seq_len), BF16),
        scratch_shapes=[pltpu.VMEM((2, seq_len + past, tq), BF16), pltpu.VMEM((MLA_HEADS, tq), F32),
                        pltpu.VMEM((MLA_HEADS, LANE), F32)],
        compiler_params=_params(2),
        name=name,
    )(*args)


_RET_PAIR = 2
_RET_PW = _RET_PAIR * RET_DK
RET_UNROLL = 8
RET_BODY_UNITS = 16


def _ret_kernel(*refs, n_local, n_chunks, unroll, has_s0, has_prev, want_final, layer_slot):
    refs = list(refs)
    ld_ref, q_ref, k_ref, v_ref, g_ref = refs[:5]
    refs = refs[5:]
    s0_ref = refs.pop(0) if has_s0 else None
    if has_prev:
        refs.pop(0)
    o_ref = refs.pop(0)
    sfin_ref = refs.pop(0) if want_final else None
    cross_ref, sf_ref, sb_ref = refs
    seq_len = n_chunks * CHUNK

    ii = lax.broadcasted_iota(jnp.int32, (CHUNK, CHUNK), 0).astype(F32)
    jj = lax.broadcasted_iota(jnp.int32, (CHUNK, CHUNK), 1).astype(F32)
    rel = ii - jj
    consts = []
    for h in range(_RET_PAIR):
        lg_f = -jnp.exp(ld_ref[0, h])
        lg_b = -jnp.exp(ld_ref[1, h])
        mask = (jnp.where(rel >= 0, jnp.exp(lg_f * jnp.maximum(rel, 0.0)), 0.0)
                + jnp.where(rel <= 0, jnp.exp(lg_b * jnp.maximum(-rel, 0.0)), 0.0))
        consts.append(dict(
            mask=mask,
            qd_f=jnp.exp(lg_f * (ii + 1.0)), kd_f=jnp.exp(lg_f * (CHUNK - 1.0 - ii)), cd_f=jnp.exp(lg_f * CHUNK),
            qd_b=jnp.exp(lg_b * (CHUNK - ii)), kd_b=jnp.exp(lg_b * ii), cd_b=jnp.exp(lg_b * CHUNK)))
    for s in range(n_local):
        for h in range(_RET_PAIR):
            if has_s0:
                sf_ref[s, h] = s0_ref[s, 0, h]
                sb_ref[s, h] = s0_ref[s, 1, h]
            else:
                sf_ref[s, h] = jnp.zeros((RET_DK, RET_DV), F32)
                sb_ref[s, h] = jnp.zeros((RET_DK, RET_DV), F32)

    units = [(s, h, u) for s in range(n_local) for h in range(_RET_PAIR) for u in range(unroll)]

    def rows_of(s, n):
        start = s * seq_len + n * CHUNK
        if isinstance(start, int):
            return slice(start, start + CHUNK)
        return pl.ds(pl.multiple_of(start, CHUNK), CHUNK)

    def head_lanes(h):
        return slice(h * RET_DK, (h + 1) * RET_DK)

    def bwd_block(t):
        loaded, kv = {}, {}
        for (s, h, u) in units:
            rows, hs = rows_of(s, n_chunks - 1 - (t * unroll + u)), head_lanes(h)
            loaded[s, h, u] = (rows, hs, q_ref[rows, hs])
            kv[s, h, u] = _dot_tn((k_ref[rows, hs].astype(F32) * consts[h]["kd_b"]).astype(BF16), v_ref[rows, hs])
        for s in range(n_local):
            for h in range(_RET_PAIR):
                c = consts[h]
                state = sb_ref[s, h]
                for u in range(unroll):
                    rows, hs, q = loaded[s, h, u]
                    cross_ref[rows, hs] = _dot(q, state.astype(BF16)) * c["qd_b"]
                    state = state * c["cd_b"] + kv[s, h, u]
                sb_ref[s, h] = state

    def fwd_block(t):
        loaded = {}
        for (s, h, u) in units:
            rows, hs = rows_of(s, t * unroll + u), head_lanes(h)
            loaded[s, h, u] = (rows, hs, q_ref[rows, hs], k_ref[rows, hs], v_ref[rows, hs])
        qk, kv = {}, {}
        for key in units:
            rows, hs, q, k, v = loaded[key]
            qk[key] = _dot_nt(q, k)
            kv[key] = _dot_tn((k.astype(F32) * consts[key[1]]["kd_f"]).astype(BF16), v)
        inner, cross = {}, {}
        for s in range(n_local):
            for h in range(_RET_PAIR):
                c = consts[h]
                state = sf_ref[s, h]
                for u in range(unroll):
                    rows, hs, q, k, v = loaded[s, h, u]
                    inner[s, h, u] = _dot((qk[s, h, u] * c["mask"]).astype(BF16), v)
                    cross[s, h, u] = _dot(q, state.astype(BF16))
                    state = state * c["cd_f"] + kv[s, h, u]
                sf_ref[s, h] = state
        for key in units:
            rows, hs = loaded[key][:2]
            o = inner[key] + cross[key] * consts[key[1]]["qd_f"] + cross_ref[rows, hs]
            gate = g_ref[rows, hs]
            o_ref[rows, hs] = (_rms(o) * (gate * jax.nn.sigmoid(gate))).astype(o_ref.dtype)

    n_blocks = n_chunks // unroll
    if n_blocks == 1:
        bwd_block(0)
        fwd_block(0)
    else:
        def loop_body(block):
            def body(t, carry):
                block(t)
                return carry
            return body
        lax.fori_loop(0, n_blocks, loop_body(bwd_block), 0)
        lax.fori_loop(0, n_blocks, loop_body(fwd_block), 0)
    if want_final:
        if has_prev:
            slot = sfin_ref
        else:
            for other in range(sfin_ref.shape[1]):
                if other != layer_slot:
                    sfin_ref[:, other] = jnp.zeros(sfin_ref.shape[:1] + sfin_ref.shape[2:], F32)
            slot = sfin_ref.at[:, layer_slot]
        for s in range(n_local):
            for h in range(_RET_PAIR):
                slot[s, 0, h] = sf_ref[s, h]
                slot[s, 1, h] = sb_ref[s, h]


def _retention(name, j, ld, rq, rk, rv, rg, n_seq, seq_len, s0=None, want_final=False, finals=None):
    pairs = RET_HEADS // _RET_PAIR
    n_chunks = seq_len // CHUNK
    n_layers = ld.shape[0]
    unroll = min(RET_UNROLL, n_chunks)
    n_local = max(1, RET_BODY_UNITS // n_chunks)
    assert n_chunks % unroll == 0 and n_seq % n_local == 0
    seq = pl.BlockSpec((n_local * seq_len, _RET_PW), lambda b, p: (b, p))
    state = pl.BlockSpec((n_local, None, 2, _RET_PAIR, RET_DK, RET_DV), lambda b, p: (b, j, 0, p, 0, 0))
    operands = [(ld, pl.BlockSpec((None, 2, _RET_PAIR, CHUNK, CHUNK), lambda b, p: (j, 0, p, 0, 0))),
                (rq, seq), (rk, seq), (rv, seq), (rg, seq)]
    if s0 is not None:
        operands.append((s0, state))
    aliases = {}
    if finals is not None:
        aliases = {len(operands): 1}
        operands.append((finals, pl.BlockSpec(memory_space=pl.ANY)))
    out_specs = [seq]
    out_shape = [jax.ShapeDtypeStruct((n_seq * seq_len, _RW), BF16)]
    if want_final:
        all_slots = pl.BlockSpec((n_local, n_layers, 2, _RET_PAIR, RET_DK, RET_DV), lambda b, p: (b, 0, 0, p, 0, 0))
        out_specs.append(state if finals is not None else all_slots)
        out_shape.append(jax.ShapeDtypeStruct((n_seq, n_layers, 2, RET_HEADS, RET_DK, RET_DV), F32))
    res = _call(
        functools.partial(_ret_kernel, n_local=n_local, n_chunks=n_chunks, unroll=unroll, has_s0=s0 is not None,
                          has_prev=finals is not None, want_final=want_final, layer_slot=j),
        operands,
        grid=(n_seq // n_local, pairs),
        out_specs=out_specs,
        out_shape=out_shape,
        input_output_aliases=aliases,
        scratch_shapes=[pltpu.VMEM((n_local * seq_len, _RET_PW), F32),
                        pltpu.VMEM((n_local, _RET_PAIR, RET_DK, RET_DV), F32),
                        pltpu.VMEM((n_local, _RET_PAIR, RET_DK, RET_DV), F32)],
        compiler_params=_params(2),
        name=name,
    )
    return res if want_final else res[0]


def _merge_ffn_kernel(x_ref, mod_ref, oat_ref, ret_ref, woa_ref, wor_ref, g_ref, wg_ref, wu_ref, wd_ref, o_ref):
    n_sub = x_ref.shape[0] // FFN_SUB
    mixed = []
    for i in range(n_sub):
        rows = _sub_rows(i)
        y = _dot_tn(oat_ref[:, rows], woa_ref[...]) + _dot(ret_ref[rows, :], wor_ref[...])
        mixed.append(x_ref[rows, :] + mod_ref[0, 0, 2:3, :] * y)

    def emit(i, y):
        o_ref[_sub_rows(i), :] = y

    _swiglu_pipeline(lambda i: mixed[i], n_sub, mod_ref, g_ref, wg_ref, wu_ref, wd_ref, emit)


def _merge_ffn(grp, layer, j, x, mods, o_attn_t, ret, w):
    t, d = x.shape
    aw = o_attn_t.shape[0]
    assert aw == ret.shape[1]
    return _call(
        _merge_ffn_kernel,
        [(x, grp.rows(d)), (mods, grp.mods(layer, d)), (o_attn_t, grp.cols(aw)), (ret, grp.rows(aw)),
         _slab(w["o_ab"], j, rows=(0, aw)), _slab(w["o_ab"], j, rows=(1, aw)), _slab(w["norm_g"], layer, 1),
         _slab(w["ffn_gate"], layer), _slab(w["ffn_up"], layer), _slab(w["ffn_down"], layer)],
        grid=(grp.n_tiles,),
        out_specs=grp.rows(d),
        out_shape=jax.ShapeDtypeStruct((t, d), F32),
        compiler_params=_params(1),
        name=f"merge_ffn_{grp.name}_{layer}",
    )


def _rope_table(dec_seq, tm):
    f32 = np.float32
    rows = dec_seq // GRID_W
    row = np.repeat(np.arange(rows), GRID_W).astype(f32)
    col = np.tile(np.arange(GRID_W), rows).astype(f32)
    half = MLA_ROPE // 2
    inv = (1.0 / np.power(f32(ROPE_BASE), np.arange(0, half, 2, dtype=f32) / f32(half))).astype(f32)
    ang = np.stack([row[:, None] * inv, col[:, None] * inv], axis=1)
    ang = np.stack([ang, ang], axis=2).reshape(dec_seq, MLA_ROPE)
    cos, sin = np.cos(ang).astype(f32), np.sin(ang).astype(f32)
    lo = MLA_NOPE + MLA_ROPE
    ones, zeros = np.ones((dec_seq, lo), f32), np.zeros((dec_seq, lo), f32)
    pos = np.concatenate([ones, cos, zeros, sin, zeros, cos, zeros, sin], axis=1)
    flat_k = np.concatenate([np.zeros((tm, MLA_NOPE), f32), np.ones((tm, MLA_ROPE), f32),
                             np.zeros((tm, MLA_ROPE), f32)], axis=1)
    flat = np.concatenate([np.ones((tm, LANE), f32), np.zeros((tm, LANE), f32), flat_k,
                           np.zeros((tm, LANE), f32)], axis=1)
    return jnp.asarray(np.concatenate([flat, pos], axis=0))


def _layout_w_in(w):
    w = w.astype(BF16)
    d = w.shape[0]
    o = 0
    parts = {}
    for name, width in (("cq", Q_LORA), ("ckv", KV_LORA), ("kr", MLA_ROPE), ("rq", _RW), ("rk", _RW),
                        ("rv", _RW), ("rg", _RW)):
        parts[name] = w[:, o:o + width]
        o += width
    kr = parts["kr"]
    kr_a = jnp.concatenate([jnp.zeros((d, MLA_NOPE), w.dtype), kr, kr], axis=1)
    kr_b = jnp.concatenate([jnp.zeros((d, MLA_NOPE + MLA_ROPE), w.dtype), _rotate_half_axial(kr)], axis=1)
    return jnp.concatenate([parts["cq"], parts["ckv"], parts["rq"], parts["rk"], parts["rv"], parts["rg"],
                            kr_a, kr_b], axis=1).astype(BF16)


def _layout_w_uq(w):
    w = w.astype(BF16)
    r = w.shape[0]
    wh = w.reshape(r, MLA_HEADS, MLA_NOPE + MLA_ROPE)
    nope, rope = wh[..., :MLA_NOPE], wh[..., MLA_NOPE:]
    main = jnp.concatenate([nope, rope, rope], axis=-1).reshape(r, _HW)
    rot = jnp.concatenate([jnp.zeros(nope.shape[:2] + (MLA_NOPE + MLA_ROPE,), w.dtype),
                           _rotate_half_axial(rope)], axis=-1).reshape(r, _HW)
    return jnp.concatenate([main, rot], axis=1).astype(BF16)


def _layout_w_ukv(w):
    w = w.astype(BF16)
    r = w.shape[0]
    wh = w.reshape(r, MLA_HEADS, MLA_NOPE + MLA_V)
    kn, v = wh[..., :MLA_NOPE], wh[..., MLA_NOPE:]
    wuk = jnp.concatenate([kn, jnp.zeros((r, MLA_HEADS, LANE - MLA_NOPE), w.dtype)], axis=-1).reshape(r, _HW)
    wv = jnp.concatenate([v, jnp.zeros((r, MLA_HEADS, LANE - MLA_V), w.dtype)], axis=-1).reshape(r, _HW)
    return wuk.astype(BF16), wv.T.astype(BF16)


def kernel(x_prompt, x_sample, cache_mla_ckv, cache_mla_krope, state_ret, c, c_ctx, w_ada, b_ada, norm_g, w_in_ab, q_norm_g, w_uq, kv_norm_g, w_ukv, ret_log_decay, w_o_ab, w_in_c, ln_g_c, ln_b_c, w_s_c, b_s_c, w_out_c, w_ffn_gate, w_ffn_up, w_ffn_down, final_norm_g):
    batch, seq, d = x_prompt.shape
    dec_batch, dec_seq, _ = x_sample.shape
    depth = w_ada.shape[0]
    assert depth % 2 == 0
    past = cache_mla_ckv.shape[2]
    groups = [_Group("p", batch, seq, 0, positional=False), _Group("s", dec_batch, dec_seq, 1, positional=True)]
    ffn_groups = [grp.with_tile(FFN_ROW_TILE) for grp in groups]
    xs = [x_prompt.reshape(batch * seq, d), x_sample.reshape(dec_batch * dec_seq, d)]

    cond8 = jnp.zeros((8, d), F32).at[0].set(c_ctx).at[1:1 + dec_batch].set(c)
    mods = _ada_mods(cond8, w_ada, b_ada)

    table = _rope_table(dec_seq, ROW_TILE)
    cache_kr_padded = jnp.pad(cache_mla_krope, ((0, 0), (0, 0), (0, 0), (MLA_NOPE, LANE - MLA_NOPE - MLA_ROPE)))
    n_ab = w_in_ab.shape[0]
    gw = w_out_c.shape[1] // GM_GROUPS
    w = dict(
        norm_g=norm_g.reshape(depth, 2, 1, d), final_g=final_norm_g.reshape(1, d),
        ffn_gate=w_ffn_gate.astype(BF16), ffn_up=w_ffn_up.astype(BF16), ffn_down=w_ffn_down.astype(BF16),
        q_norm_g=q_norm_g.reshape(n_ab, 1, -1), kv_norm_g=kv_norm_g.reshape(n_ab, 1, -1), o_ab=w_o_ab.astype(BF16),
        in_c=w_in_c.astype(BF16), out_c=w_out_c.astype(BF16), s_c=w_s_c.astype(BF16),
        ln_g_c=ln_g_c[:, None, :], ln_b_c=ln_b_c[:, None, :],
        bs_c=jnp.broadcast_to(b_s_c[:, :, :, None], b_s_c.shape + (gw,)))
    ld = jnp.broadcast_to(ret_log_decay[:, :, :, None, None], ret_log_decay.shape + (CHUNK, CHUNK))

    ckv_out, kr_out, finals = [], [], None
    for l in range(depth):
        j = l // 2
        last = l == depth - 1
        if l % 2 == 0:
            win, wuq = _layout_w_in(w_in_ab[j]), _layout_w_uq(w_uq[j])
            wuk, wvt = _layout_w_ukv(w_ukv[j])
            kc, vtc = _ctxkv(j, cache_mla_ckv, cache_kr_padded, wuk, wvt)
            for gi, grp in enumerate(groups):
                prompt = gi == 0
                outs = _abproj(grp, l, j, xs[gi], mods, w, table, win, wuq, wuk, wvt, with_cache=prompt)
                q, k, vt, rq, rk, rv, rg = outs[:7]
                if prompt:
                    oat = _attention(f"attn_p_{l}", q, k, vt, grp.n_seq, grp.seq_len)
                    ret, finals = _retention(f"ret_p_{l}", j, ld, rq, rk, rv, rg, grp.n_seq, grp.seq_len,
                                             want_final=True, finals=finals)
                    ckv_out.append(outs[7].reshape(batch, seq, KV_LORA))
                    kr_out.append(outs[8][:, MLA_NOPE:MLA_NOPE + MLA_ROPE].reshape(batch, seq, MLA_ROPE))
                else:
                    oat = _attention(f"attn_s_{l}", q, k, vt, grp.n_seq, grp.seq_len, ctx=(kc, vtc, past))
                    ret = _retention(f"ret_s_{l}", j, ld, rq, rk, rv, rg, grp.n_seq, grp.seq_len, s0=state_ret)
                xs[gi] = _merge_ffn(ffn_groups[gi], l, j, xs[gi], mods, oat, ret, w)
        else:
            for gi, grp in enumerate(groups):
                x1 = _cmix(grp, l, j, xs[gi], mods, w)
                xs[gi] = _ffn(ffn_groups[gi], l, x1, mods, w, final_norm=last)

    y_prompt = xs[0].reshape(batch, seq, d)
    y_sample = xs[1].reshape(dec_batch, dec_seq, d)
    return (y_prompt, y_sample, jnp.stack(ckv_out, axis=1), jnp.stack(kr_out, axis=1), finals)
```

```python
import functools
import math

import jax
import jax.numpy as jnp
import numpy as np
from jax import lax
from jax.experimental import pallas as pl
from jax.experimental.pallas import tpu as pltpu

F32 = jnp.float32
BF16 = jnp.bfloat16
EPS = 1e-6

MLA_HEADS = 8
MLA_NOPE = 64
MLA_ROPE = 32
MLA_V = 64
Q_LORA = 256
KV_LORA = 128
ROPE_BASE = 10000.0
GRID_W = 64
RET_HEADS = 4
RET_DK = 128
RET_DV = 128
CHUNK = 128
GM_GROUPS = 8

LANE = 128
ROW_TILE = 512
ATTN_Q_TILE = 512
KEY_CHUNK = 256
SCORE_CHUNKS = 4
SAFE_SHIFT = 50.0
BF16_SUBLANES = 16
PV_ROWS = -(-(MLA_V + 1) // BF16_SUBLANES) * BF16_SUBLANES
VMEM_LIMIT = 56 * 1024 * 1024


def _params(n_axes):
    return pltpu.CompilerParams(dimension_semantics=("arbitrary",) * n_axes,
                                vmem_limit_bytes=VMEM_LIMIT)


def _resident(shape):
    nd = len(shape)
    return pl.BlockSpec(shape, lambda *_: (0,) * nd, pipeline_mode=pl.Buffered(1))


def _slab(arr, *lead, rows=None):
    shape = arr.shape[len(lead):]
    first = 0
    if rows is not None:
        first, n = rows
        shape = (n,) + shape[1:]
    idx = tuple(lead) + (first,) + (0,) * (len(shape) - 1)
    return arr, pl.BlockSpec((None,) * len(lead) + shape, lambda *_: idx, pipeline_mode=pl.Buffered(1))


def _call(kernel_fn, operands, **kw):
    return pl.pallas_call(kernel_fn, in_specs=[s for _, s in operands], **kw)(*[a for a, _ in operands])


def _rms(x):
    return x * lax.rsqrt(jnp.mean(x * x, axis=-1, keepdims=True) + EPS)


def _dot(a, b):
    return jnp.dot(a, b, preferred_element_type=F32)


def _dot_nt(a, b):
    return lax.dot_general(a, b, (((1,), (1,)), ((), ())), preferred_element_type=F32)


def _dot_tn(a, b):
    return lax.dot_general(a, b, (((0,), (0,)), ((), ())), preferred_element_type=F32)


def _tile_heads(x, n):
    return jnp.concatenate([x] * n, axis=1)


def _rotate_half_axial(x):
    xs = x.reshape(x.shape[:-1] + (2, 2, MLA_ROPE // 4))
    return jnp.stack([-xs[..., 1, :], xs[..., 0, :]], axis=-2).reshape(x.shape)


def _ada_kernel(cond_ref, w_ref, b_ref, o_ref):
    c = cond_ref[...]
    a = (c * jax.nn.sigmoid(c)).astype(BF16)
    o_ref[0] = _dot(a, w_ref[0].astype(BF16)) + b_ref[0]


def _ada_mods(cond8, w_ada, b_ada):
    depth, d, n = w_ada.shape
    tn = n // 4
    out = pl.pallas_call(
        _ada_kernel,
        grid=(depth, n // tn),
        in_specs=[pl.BlockSpec((8, d), lambda l, j: (0, 0)),
                  pl.BlockSpec((1, d, tn), lambda l, j: (l, 0, j)),
                  pl.BlockSpec((1, 1, tn), lambda l, j: (l, 0, j))],
        out_specs=pl.BlockSpec((1, 8, tn), lambda l, j: (l, 0, j)),
        out_shape=jax.ShapeDtypeStruct((depth, 8, n), F32),
        compiler_params=_params(2),
        name="ada_mod",
    )(cond8, w_ada, b_ada.reshape(depth, 1, n))
    return out.reshape(depth, 8, 6, d)


class _Group:
    def __init__(self, name, n_seq, seq_len, cond_base, positional, tm=ROW_TILE):
        assert seq_len % tm == 0 or tm % seq_len == 0
        self.name = name
        self.n_seq, self.seq_len = n_seq, seq_len
        self.n_rows = n_seq * seq_len
        self.tm = tm
        self.n_tiles = self.n_rows // tm
        self.cond_base = cond_base
        self.tiles_per_cond = seq_len // tm if positional else self.n_tiles
        self.positional = positional

    def with_tile(self, tm):
        return _Group(self.name, self.n_seq, self.seq_len, self.cond_base, self.positional, tm)

    def cond(self, i):
        return self.cond_base + i // self.tiles_per_cond

    def table(self, i):
        return 1 + i % self.tiles_per_cond if self.positional else 0

    def rows(self, width):
        return pl.BlockSpec((self.tm, width), lambda i: (i, 0))

    def cols(self, height):
        return pl.BlockSpec((height, self.tm), lambda i: (0, i))

    def mods(self, layer, d):
        return pl.BlockSpec((1, 1, 6, d), lambda i: (layer, self.cond(i), 0, 0))


def _modulated(x, g, shift, scale):
    return _rms(x) * (g * (1.0 + scale)) + shift


_GELU_C = math.sqrt(2.0 / math.pi)


def _gelu_tanh(x):
    half = 0.5 * x
    return half + half * jnp.tanh(x * (_GELU_C + (_GELU_C * 0.044715) * (x * x)))


FFN_ROW_TILE = 1024
FFN_SUB = 256


def _swiglu_pipeline(x_of, n_sub, mod_ref, g_ref, wg_ref, wu_ref, wd_ref, emit):
    def project(i):
        x = x_of(i)
        hb = _modulated(x, g_ref[...], mod_ref[0, 0, 3:4, :], mod_ref[0, 0, 4:5, :]).astype(BF16)
        return x, _dot(hb, wg_ref[...]), _dot(hb, wu_ref[...])

    def finish(i, x, gate, up):
        act = (gate * jax.nn.sigmoid(gate) * up).astype(BF16)
        emit(i, x + mod_ref[0, 0, 5:6, :] * _dot(act, wd_ref[...]))

    cur = project(0)
    for i in range(n_sub):
        nxt = project(i + 1) if i + 1 < n_sub else None
        finish(i, *cur)
        cur = nxt


def _sub_rows(i):
    return slice(i * FFN_SUB, (i + 1) * FFN_SUB)


def _ffn_kernel(x_ref, mod_ref, g_ref, wg_ref, wu_ref, wd_ref, fg_ref, o_ref, *, final_norm):
    def emit(i, y):
        if final_norm:
            y = _rms(y) * fg_ref[...]
        o_ref[_sub_rows(i), :] = y

    _swiglu_pipeline(lambda i: x_ref[_sub_rows(i), :], x_ref.shape[0] // FFN_SUB,
                     mod_ref, g_ref, wg_ref, wu_ref, wd_ref, emit)


def _ffn(grp, layer, x, mods, w, final_norm):
    t, d = x.shape
    return _call(
        functools.partial(_ffn_kernel, final_norm=final_norm),
        [(x, grp.rows(d)), (mods, grp.mods(layer, d)), _slab(w["norm_g"], layer, 1),
         _slab(w["ffn_gate"], layer), _slab(w["ffn_up"], layer), _slab(w["ffn_down"], layer), _slab(w["final_g"])],
        grid=(grp.n_tiles,),
        out_specs=grp.rows(d),
        out_shape=jax.ShapeDtypeStruct((t, d), F32),
        compiler_params=_params(1),
        name=f"ffn_{grp.name}_{layer}",
    )


CMIX_SUB = 256


def _cmix_kernel(x_ref, mod_ref, g_ref, win_ref, lng_ref, lnb_ref, ws_ref, bs_ref, wout_ref, o_ref, gated_ref):
    tm = x_ref.shape[0]
    w = lng_ref.shape[1]
    gw = w // GM_GROUPS

    def project(i):
        x = x_ref[i * CMIX_SUB:(i + 1) * CMIX_SUB, :]
        hb = _modulated(x, g_ref[...], mod_ref[0, 0, 0:1, :], mod_ref[0, 0, 1:2, :]).astype(BF16)
        return _dot(hb, win_ref[...])

    def gate(i, uv):
        uv = _gelu_tanh(uv)
        u = uv[:, :w]
        v = uv[:, w:]
        mu = jnp.mean(v, axis=-1, keepdims=True)
        vc = v - mu
        var = jnp.mean(vc * vc, axis=-1, keepdims=True)
        vb = (vc * lax.rsqrt(var + EPS) * lng_ref[...] + lnb_ref[...]).astype(BF16)
        for r in range(CMIX_SUB // CHUNK):
            rs = slice(r * CHUNK, (r + 1) * CHUNK)
            out_rows = slice(i * CMIX_SUB + r * CHUNK, i * CMIX_SUB + (r + 1) * CHUNK)
            for gi in range(GM_GROUPS):
                cs = slice(gi * gw, (gi + 1) * gw)
                mixed = _dot(ws_ref[gi], vb[rs, cs]) + bs_ref[gi]
                gated_ref[out_rows, cs] = (u[rs, cs] * mixed).astype(BF16)

    def project_out(i):
        rows = slice(i * CMIX_SUB, (i + 1) * CMIX_SUB)
        o_ref[rows, :] = x_ref[rows, :] + mod_ref[0, 0, 2:3, :] * _dot(gated_ref[rows, :], wout_ref[...])

    n_sub = tm // CMIX_SUB
    uv = project(0)
    for i in range(n_sub):
        uv_next = project(i + 1) if i + 1 < n_sub else None
        gate(i, uv)
        project_out(i)
        uv = uv_next


def _cmix(grp, layer, j, x, mods, w):
    t, d = x.shape
    width = w["out_c"].shape[1]
    return _call(
        _cmix_kernel,
        [(x, grp.rows(d)), (mods, grp.mods(layer, d)), _slab(w["norm_g"], layer, 0), _slab(w["in_c"], j),
         _slab(w["ln_g_c"], j), _slab(w["ln_b_c"], j), _slab(w["s_c"], j), _slab(w["bs_c"], j), _slab(w["out_c"], j)],
        grid=(grp.n_tiles,),
        out_specs=grp.rows(d),
        out_shape=jax.ShapeDtypeStruct((t, d), F32),
        scratch_shapes=[pltpu.VMEM((grp.tm, width), BF16)],
        compiler_params=_params(1),
        name=f"cmix_{grp.name}_{layer}",
    )


_HW = MLA_HEADS * LANE
_RW = RET_HEADS * RET_DK
_C_CQ = 0
_C_CKV = _C_CQ + Q_LORA
_C_RQ = _C_CKV + KV_LORA
_C_RK = _C_RQ + _RW
_C_RV = _C_RK + _RW
_C_RG = _C_RV + _RW
_C_KRA = _C_RG + _RW
_C_KRB = _C_KRA + LANE
_C_END = _C_KRB + LANE

_Q_SCALE = (MLA_NOPE + MLA_ROPE) ** -0.5 * math.log2(math.e)


def _value_rows_t(ckv_b, wvt_ref):
    vt = _dot_nt(wvt_ref[...], ckv_b)
    row = lax.broadcasted_iota(jnp.int32, vt.shape, 0)
    return jnp.where((row & (LANE - 1)) == MLA_V, 1.0, vt)


def _abproj_kernel(*refs, with_cache):
    (x_ref, mod_ref, g_ref, tab_ref, win_ref, qg_ref, wuq_ref, kvg_ref, wuk_ref, wvt_ref,
     q_ref, k_ref, vt_ref, rq_ref, rk_ref, rv_ref, rg_ref) = refs[:17]
    x = x_ref[...]
    hb = _modulated(x, g_ref[...], mod_ref[0, 0, 0:1, :], mod_ref[0, 0, 1:2, :]).astype(BF16)
    p = _dot(hb, win_ref[...])
    rq_ref[...] = p[:, _C_RQ:_C_RK].astype(BF16)
    rk_ref[...] = (p[:, _C_RK:_C_RV] * (RET_DK ** -0.5)).astype(BF16)
    rv_ref[...] = p[:, _C_RV:_C_RG].astype(BF16)
    rg_ref[...] = p[:, _C_RG:_C_KRA]
    kr_a = p[:, _C_KRA:_C_KRB]
    kr_b = p[:, _C_KRB:_C_END]

    tab = tab_ref[...]
    cos_q, sin_q = tab[:, 0:LANE], tab[:, LANE:2 * LANE]
    cos_k, sin_k = tab[:, 2 * LANE:3 * LANE], tab[:, 3 * LANE:4 * LANE]

    cqn = (_rms(p[:, _C_CQ:_C_CKV]) * qg_ref[...]).astype(BF16)
    q2 = _dot(cqn, wuq_ref[...])
    q = q2[:, :_HW] * _tile_heads(cos_q, MLA_HEADS) + q2[:, _HW:] * _tile_heads(sin_q, MLA_HEADS)
    q_ref[...] = (q * _Q_SCALE).astype(BF16)

    ckv = _rms(p[:, _C_CKV:_C_RQ]) * kvg_ref[...]
    ckv_b = ckv.astype(BF16)
    k_rope = kr_a * cos_k + kr_b * sin_k
    k_ref[...] = (_dot(ckv_b, wuk_ref[...]) + _tile_heads(k_rope, MLA_HEADS)).astype(BF16)
    vt_ref[...] = _value_rows_t(ckv_b, wvt_ref).astype(BF16)
    if with_cache:
        ckv_ref, kr_ref = refs[17:]
        ckv_ref[...] = ckv
        kr_ref[...] = kr_a


def _abproj(grp, layer, j, x, mods, w, table, win, wuq, wuk, wvt, with_cache):
    t, d = x.shape
    row_outs = [(_HW, BF16), (_HW, BF16), None, (_RW, BF16), (_RW, BF16), (_RW, BF16), (_RW, F32)]
    if with_cache:
        row_outs += [(KV_LORA, F32), (LANE, F32)]
    out_specs = [grp.cols(_HW) if o is None else grp.rows(o[0]) for o in row_outs]
    out_shape = [jax.ShapeDtypeStruct((_HW, t), BF16) if o is None else jax.ShapeDtypeStruct((t, o[0]), o[1])
                 for o in row_outs]
    return _call(
        functools.partial(_abproj_kernel, with_cache=with_cache),
        [(x, grp.rows(d)), (mods, grp.mods(layer, d)), _slab(w["norm_g"], layer, 0),
         (table, pl.BlockSpec((grp.tm, 4 * LANE), lambda i: (grp.table(i), 0))),
         _slab(win), _slab(w["q_norm_g"], j), _slab(wuq), _slab(w["kv_norm_g"], j), _slab(wuk), _slab(wvt)],
        grid=(grp.n_tiles,),
        out_specs=out_specs,
        out_shape=out_shape,
        compiler_params=_params(1),
        name=f"abproj_{grp.name}_{layer}",
    )


def _ctxkv_kernel(ckv_ref, kr_ref, wuk_ref, wvt_ref, k_ref, vt_ref):
    ckv_b = ckv_ref[...].astype(BF16)
    k_ref[...] = (_dot(ckv_b, wuk_ref[...]) + _tile_heads(kr_ref[...], MLA_HEADS)).astype(BF16)
    vt_ref[...] = _value_rows_t(ckv_b, wvt_ref).astype(BF16)


def _ctxkv(layer_idx, cache_ckv, cache_kr_padded, wuk, wvt):
    b, _, past, kvl = cache_ckv.shape
    return pl.pallas_call(
        _ctxkv_kernel,
        grid=(b,),
        in_specs=[pl.BlockSpec((None, None, past, kvl), lambda i: (i, layer_idx, 0, 0)),
                  pl.BlockSpec((None, None, past, LANE), lambda i: (i, layer_idx, 0, 0)),
                  _resident(wuk.shape), _resident(wvt.shape)],
        out_specs=[pl.BlockSpec((past, _HW), lambda i: (i, 0)), pl.BlockSpec((_HW, past), lambda i: (0, i))],
        out_shape=[jax.ShapeDtypeStruct((b * past, _HW), BF16), jax.ShapeDtypeStruct((_HW, b * past), BF16)],
        compiler_params=_params(1),
        name=f"ctxkv_{layer_idx}",
    )(cache_ckv, cache_kr_padded, wuk, wvt)


def _attn_kernel(*refs, n_main, n_ctx):
    if n_ctx:
        q_ref, k_ref, vt_ref, kc_ref, vtc_ref, o_ref, p_scr, shift_scr, knorm_scr = refs
    else:
        q_ref, k_ref, vt_ref, o_ref, p_scr, shift_scr, knorm_scr = refs
        kc_ref = vtc_ref = None
    tq = q_ref.shape[0]
    kc = KEY_CHUNK
    chunks = [(kc_ref, vtc_ref, j) for j in range(n_ctx)] + [(k_ref, vt_ref, j) for j in range(n_main)]
    nc = len(chunks)
    score_starts = list(range(0, n_ctx, SCORE_CHUNKS)) + list(range(n_ctx, nc, SCORE_CHUNKS))
    key_refs = [r for r in (kc_ref, k_ref) if r is not None]

    def head_lanes(h):
        return slice(h * LANE, (h + 1) * LANE)

    def score_block(h, c0):
        kref, _, j = chunks[c0]
        n = min(SCORE_CHUNKS, (n_ctx if c0 < n_ctx else nc) - c0)
        return n, _dot_nt(kref[j * kc:(j + n) * kc, head_lanes(h)], q_ref[:, head_lanes(h)])

    @pl.when(pl.program_id(1) == 0)
    def _():
        for h in range(MLA_HEADS):
            best = None
            for kref in key_refs:
                kf = kref[:, head_lanes(h)].astype(F32)
                n2 = jnp.max(jnp.sum(kf * kf, axis=1, keepdims=True), axis=0, keepdims=True)
                best = n2 if best is None else jnp.maximum(best, n2)
            knorm_scr[h:h + 1, :] = jnp.broadcast_to(best, (1, LANE))

    ones = jnp.ones((8, LANE), BF16)
    worst = None
    for h in range(MLA_HEADS):
        qf = q_ref[:, head_lanes(h)].astype(F32)
        qn2 = _dot_nt(ones, (qf * qf).astype(BF16))[0:1, :]
        bound = jnp.sqrt(qn2 * knorm_scr[h:h + 1, 0:1]) * 1.01 + 1e-3
        shift_scr[h:h + 1, :] = bound
        w = jnp.max(bound, axis=1, keepdims=True)
        worst = w if worst is None else jnp.maximum(worst, w)

    @pl.when(worst[0, 0] > SAFE_SHIFT)
    def _():
        for h in range(MLA_HEADS):
            mpart = None
            for c0 in score_starts:
                n, s = score_block(h, c0)
                part = jnp.max(s.reshape(n * kc // 8, 8, tq), axis=0)
                mpart = part if mpart is None else jnp.maximum(mpart, part)
            shift_scr[h:h + 1, :] = jnp.max(mpart, axis=0, keepdims=True)

    def score_exp_block(h, c0):
        n, s = score_block(h, c0)
        p_scr[h % 2, c0 * kc:(c0 + n) * kc, :] = jnp.exp2(s - shift_scr[h:h + 1, :]).astype(BF16)

    def value_chunk(h, c, acc):
        _, vref, j = chunks[c]
        d = _dot(vref[h * LANE:h * LANE + PV_ROWS, j * kc:(j + 1) * kc], p_scr[h % 2, c * kc:(c + 1) * kc, :])
        return d if acc is None else acc + d

    for h in range(MLA_HEADS + 1):
        acc = None
        for c in range(nc):
            if h < MLA_HEADS and c in score_starts:
                score_exp_block(h, c)
            if h >= 1:
                acc = value_chunk(h - 1, c, acc)
        if h >= 1:
            inv = 1.0 / acc[MLA_V:MLA_V + 1, :]
            o_ref[(h - 1) * MLA_V:h * MLA_V, :] = (acc[:MLA_V, :] * inv).astype(o_ref.dtype)


def _attention(name, q, k, vt, n_seq, seq_len, ctx=None):
    tq = min(ATTN_Q_TILE, seq_len)
    qt = seq_len // tq
    ow = MLA_HEADS * MLA_V
    mode = {}
    in_specs = [pl.BlockSpec((tq, _HW), lambda b, i: (b * qt + i, 0)),
                pl.BlockSpec((seq_len, _HW), lambda b, i: (b, 0), **mode),
                pl.BlockSpec((_HW, seq_len), lambda b, i: (0, b), **mode)]
    args = [q, k, vt]
    past = 0
    if ctx is not None:
        kc, vtc, past = ctx
        in_specs += [pl.BlockSpec((past, _HW), lambda b, i: (b, 0), **mode),
                     pl.BlockSpec((_HW, past), lambda b, i: (0, b), **mode)]
        args += [kc, vtc]
    assert seq_len % KEY_CHUNK == 0 and past % KEY_CHUNK == 0
    return pl.pallas_call(
        functools.partial(_attn_kernel, n_main=seq_len // KEY_CHUNK, n_ctx=past // KEY_CHUNK),
        grid=(n_seq, qt),
        in_specs=in_specs,
        out_specs=pl.BlockSpec((ow, tq), lambda b, i: (0, b * qt + i)),
        out_shape=jax.ShapeDtypeStruct((ow, n_seq * seq_len), BF16),
        scratch_shapes=[pltpu.VMEM((2, seq_len + past, tq), BF16), pltpu.VMEM((MLA_HEADS, tq), F32),
                        pltpu.VMEM((MLA_HEADS, LANE), F32)],
        compiler_params=_params(2),
        name=name,
    )(*args)


_RET_PAIR = 2
_RET_PW = _RET_PAIR * RET_DK
RET_UNROLL = 8
RET_BODY_UNITS = 16


def _ret_kernel(*refs, n_local, n_chunks, unroll, has_s0, has_prev, want_final, layer_slot):
    refs = list(refs)
    ld_ref, q_ref, k_ref, v_ref, g_ref = refs[:5]
    refs = refs[5:]
    s0_ref = refs.pop(0) if has_s0 else None
    if has_prev:
        refs.pop(0)
    o_ref = refs.pop(0)
    sfin_ref = refs.pop(0) if want_final else None
    cross_ref, sf_ref, sb_ref = refs
    seq_len = n_chunks * CHUNK

    ii = lax.broadcasted_iota(jnp.int32, (CHUNK, CHUNK), 0).astype(F32)
    jj = lax.broadcasted_iota(jnp.int32, (CHUNK, CHUNK), 1).astype(F32)
    rel = ii - jj
    consts = []
    for h in range(_RET_PAIR):
        lg_f = -jnp.exp(ld_ref[0, h])
        lg_b = -jnp.exp(ld_ref[1, h])
        mask = (jnp.where(rel >= 0, jnp.exp(lg_f * jnp.maximum(rel, 0.0)), 0.0)
                + jnp.where(rel <= 0, jnp.exp(lg_b * jnp.maximum(-rel, 0.0)), 0.0))
        consts.append(dict(
            mask=mask,
            qd_f=jnp.exp(lg_f * (ii + 1.0)), kd_f=jnp.exp(lg_f * (CHUNK - 1.0 - ii)), cd_f=jnp.exp(lg_f * CHUNK),
            qd_b=jnp.exp(lg_b * (CHUNK - ii)), kd_b=jnp.exp(lg_b * ii), cd_b=jnp.exp(lg_b * CHUNK)))
    for s in range(n_local):
        for h in range(_RET_PAIR):
            if has_s0:
                sf_ref[s, h] = s0_ref[s, 0, h]
                sb_ref[s, h] = s0_ref[s, 1, h]
            else:
                sf_ref[s, h] = jnp.zeros((RET_DK, RET_DV), F32)
                sb_ref[s, h] = jnp.zeros((RET_DK, RET_DV), F32)

    units = [(s, h, u) for s in range(n_local) for h in range(_RET_PAIR) for u in range(unroll)]

    def rows_of(s, n):
        start = s * seq_len + n * CHUNK
        if isinstance(start, int):
            return slice(start, start + CHUNK)
        return pl.ds(pl.multiple_of(start, CHUNK), CHUNK)

    def head_lanes(h):
        return slice(h * RET_DK, (h + 1) * RET_DK)

    def bwd_block(t):
        loaded, kv = {}, {}
        for (s, h, u) in units:
            rows, hs = rows_of(s, n_chunks - 1 - (t * unroll + u)), head_lanes(h)
            loaded[s, h, u] = (rows, hs, q_ref[rows, hs])
            kv[s, h, u] = _dot_tn((k_ref[rows, hs].astype(F32) * consts[h]["kd_b"]).astype(BF16), v_ref[rows, hs])
        for s in range(n_local):
            for h in range(_RET_PAIR):
                c = consts[h]
                state = sb_ref[s, h]
                for u in range(unroll):
                    rows, hs, q = loaded[s, h, u]
                    cross_ref[rows, hs] = _dot(q, state.astype(BF16)) * c["qd_b"]
                    state = state * c["cd_b"] + kv[s, h, u]
                sb_ref[s, h] = state

    def fwd_block(t):
        loaded = {}
        for (s, h, u) in units:
            rows, hs = rows_of(s, t * unroll + u), head_lanes(h)
            loaded[s, h, u] = (rows, hs, q_ref[rows, hs], k_ref[rows, hs], v_ref[rows, hs])
        qk, kv = {}, {}
        for key in units:
            rows, hs, q, k, v = loaded[key]
            qk[key] = _dot_nt(q, k)
            kv[key] = _dot_tn((k.astype(F32) * consts[key[1]]["kd_f"]).astype(BF16), v)
        inner, cross = {}, {}
        for s in range(n_local):
            for h in range(_RET_PAIR):
                c = consts[h]
                state = sf_ref[s, h]
                for u in range(unroll):
                    rows, hs, q, k, v = loaded[s, h, u]
                    inner[s, h, u] = _dot((qk[s, h, u] * c["mask"]).astype(BF16), v)
                    cross[s, h, u] = _dot(q, state.astype(BF16))
                    state = state * c["cd_f"] + kv[s, h, u]
                sf_ref[s, h] = state
        for key in units:
            rows, hs = loaded[key][:2]
            o = inner[key] + cross[key] * consts[key[1]]["qd_f"] + cross_ref[rows, hs]
            gate = g_ref[rows, hs]
            o_ref[rows, hs] = (_rms(o) * (gate * jax.nn.sigmoid(gate))).astype(o_ref.dtype)

    n_blocks = n_chunks // unroll
    if n_blocks == 1:
        bwd_block(0)
        fwd_block(0)
    else:
        def loop_body(block):
            def body(t, carry):
                block(t)
                return carry
            return body
        lax.fori_loop(0, n_blocks, loop_body(bwd_block), 0)
        lax.fori_loop(0, n_blocks, loop_body(fwd_block), 0)
    if want_final:
        if has_prev:
            slot = sfin_ref
        else:
            for other in range(sfin_ref.shape[1]):
                if other != layer_slot:
                    sfin_ref[:, other] = jnp.zeros(sfin_ref.shape[:1] + sfin_ref.shape[2:], F32)
            slot = sfin_ref.at[:, layer_slot]
        for s in range(n_local):
            for h in range(_RET_PAIR):
                slot[s, 0, h] = sf_ref[s, h]
                slot[s, 1, h] = sb_ref[s, h]


def _retention(name, j, ld, rq, rk, rv, rg, n_seq, seq_len, s0=None, want_final=False, finals=None):
    pairs = RET_HEADS // _RET_PAIR
    n_chunks = seq_len // CHUNK
    n_layers = ld.shape[0]
    unroll = min(RET_UNROLL, n_chunks)
    n_local = max(1, RET_BODY_UNITS // n_chunks)
    assert n_chunks % unroll == 0 and n_seq % n_local == 0
    seq = pl.BlockSpec((n_local * seq_len, _RET_PW), lambda b, p: (b, p))
    state = pl.BlockSpec((n_local, None, 2, _RET_PAIR, RET_DK, RET_DV), lambda b, p: (b, j, 0, p, 0, 0))
    operands = [(ld, pl.BlockSpec((None, 2, _RET_PAIR, CHUNK, CHUNK), lambda b, p: (j, 0, p, 0, 0))),
                (rq, seq), (rk, seq), (rv, seq), (rg, seq)]
    if s0 is not None:
        operands.append((s0, state))
    aliases = {}
    if finals is not None:
        aliases = {len(operands): 1}
        operands.append((finals, pl.BlockSpec(memory_space=pl.ANY)))
    out_specs = [seq]
    out_shape = [jax.ShapeDtypeStruct((n_seq * seq_len, _RW), BF16)]
    if want_final:
        all_slots = pl.BlockSpec((n_local, n_layers, 2, _RET_PAIR, RET_DK, RET_DV), lambda b, p: (b, 0, 0, p, 0, 0))
        out_specs.append(state if finals is not None else all_slots)
        out_shape.append(jax.ShapeDtypeStruct((n_seq, n_layers, 2, RET_HEADS, RET_DK, RET_DV), F32))
    res = _call(
        functools.partial(_ret_kernel, n_local=n_local, n_chunks=n_chunks, unroll=unroll, has_s0=s0 is not None,
                          has_prev=finals is not None, want_final=want_final, layer_slot=j),
        operands,
        grid=(n_seq // n_local, pairs),
        out_specs=out_specs,
        out_shape=out_shape,
        input_output_aliases=aliases,
        scratch_shapes=[pltpu.VMEM((n_local * seq_len, _RET_PW), F32),
                        pltpu.VMEM((n_local, _RET_PAIR, RET_DK, RET_DV), F32),
                        pltpu.VMEM((n_local, _RET_PAIR, RET_DK, RET_DV), F32)],
        compiler_params=_params(2),
        name=name,
    )
    return res if want_final else res[0]


def _merge_ffn_kernel(x_ref, mod_ref, oat_ref, ret_ref, woa_ref, wor_ref, g_ref, wg_ref, wu_ref, wd_ref, o_ref):
    n_sub = x_ref.shape[0] // FFN_SUB
    mixed = []
    for i in range(n_sub):
        rows = _sub_rows(i)
        y = _dot_tn(oat_ref[:, rows], woa_ref[...]) + _dot(ret_ref[rows, :], wor_ref[...])
        mixed.append(x_ref[rows, :] + mod_ref[0, 0, 2:3, :] * y)

    def emit(i, y):
        o_ref[_sub_rows(i), :] = y

    _swiglu_pipeline(lambda i: mixed[i], n_sub, mod_ref, g_ref, wg_ref, wu_ref, wd_ref, emit)


def _merge_ffn(grp, layer, j, x, mods, o_attn_t, ret, w):
    t, d = x.shape
    aw = o_attn_t.shape[0]
    assert aw == ret.shape[1]
    return _call(
        _merge_ffn_kernel,
        [(x, grp.rows(d)), (mods, grp.mods(layer, d)), (o_attn_t, grp.cols(aw)), (ret, grp.rows(aw)),
         _slab(w["o_ab"], j, rows=(0, aw)), _slab(w["o_ab"], j, rows=(1, aw)), _slab(w["norm_g"], layer, 1),
         _slab(w["ffn_gate"], layer), _slab(w["ffn_up"], layer), _slab(w["ffn_down"], layer)],
        grid=(grp.n_tiles,),
        out_specs=grp.rows(d),
        out_shape=jax.ShapeDtypeStruct((t, d), F32),
        compiler_params=_params(1),
        name=f"merge_ffn_{grp.name}_{layer}",
    )


def _rope_table(dec_seq, tm):
    f32 = np.float32
    rows = dec_seq // GRID_W
    row = np.repeat(np.arange(rows), GRID_W).astype(f32)
    col = np.tile(np.arange(GRID_W), rows).astype(f32)
    half = MLA_ROPE // 2
    inv = (1.0 / np.power(f32(ROPE_BASE), np.arange(0, half, 2, dtype=f32) / f32(half))).astype(f32)
    ang = np.stack([row[:, None] * inv, col[:, None] * inv], axis=1)
    ang = np.stack([ang, ang], axis=2).reshape(dec_seq, MLA_ROPE)
    cos, sin = np.cos(ang).astype(f32), np.sin(ang).astype(f32)
    lo = MLA_NOPE + MLA_ROPE
    ones, zeros = np.ones((dec_seq, lo), f32), np.zeros((dec_seq, lo), f32)
    pos = np.concatenate([ones, cos, zeros, sin, zeros, cos, zeros, sin], axis=1)
    flat_k = np.concatenate([np.zeros((tm, MLA_NOPE), f32), np.ones((tm, MLA_ROPE), f32),
                             np.zeros((tm, MLA_ROPE), f32)], axis=1)
    flat = np.concatenate([np.ones((tm, LANE), f32), np.zeros((tm, LANE), f32), flat_k,
                           np.zeros((tm, LANE), f32)], axis=1)
    return jnp.asarray(np.concatenate([flat, pos], axis=0))


def _layout_w_in(w):
    w = w.astype(BF16)
    d = w.shape[0]
    o = 0
    parts = {}
    for name, width in (("cq", Q_LORA), ("ckv", KV_LORA), ("kr", MLA_ROPE), ("rq", _RW), ("rk", _RW),
                        ("rv", _RW), ("rg", _RW)):
        parts[name] = w[:, o:o + width]
        o += width
    kr = parts["kr"]
    kr_a = jnp.concatenate([jnp.zeros((d, MLA_NOPE), w.dtype), kr, kr], axis=1)
    kr_b = jnp.concatenate([jnp.zeros((d, MLA_NOPE + MLA_ROPE), w.dtype), _rotate_half_axial(kr)], axis=1)
    return jnp.concatenate([parts["cq"], parts["ckv"], parts["rq"], parts["rk"], parts["rv"], parts["rg"],
                            kr_a, kr_b], axis=1).astype(BF16)


def _layout_w_uq(w):
    w = w.astype(BF16)
    r = w.shape[0]
    wh = w.reshape(r, MLA_HEADS, MLA_NOPE + MLA_ROPE)
    nope, rope = wh[..., :MLA_NOPE], wh[..., MLA_NOPE:]
    main = jnp.concatenate([nope, rope, rope], axis=-1).reshape(r, _HW)
    rot = jnp.concatenate([jnp.zeros(nope.shape[:2] + (MLA_NOPE + MLA_ROPE,), w.dtype),
                           _rotate_half_axial(rope)], axis=-1).reshape(r, _HW)
    return jnp.concatenate([main, rot], axis=1).astype(BF16)


def _layout_w_ukv(w):
    w = w.astype(BF16)
    r = w.shape[0]
    wh = w.reshape(r, MLA_HEADS, MLA_NOPE + MLA_V)
    kn, v = wh[..., :MLA_NOPE], wh[..., MLA_NOPE:]
    wuk = jnp.concatenate([kn, jnp.zeros((r, MLA_HEADS, LANE - MLA_NOPE), w.dtype)], axis=-1).reshape(r, _HW)
    wv = jnp.concatenate([v, jnp.zeros((r, MLA_HEADS, LANE - MLA_V), w.dtype)], axis=-1).reshape(r, _HW)
    return wuk.astype(BF16), wv.T.astype(BF16)


def kernel(x_prompt, x_sample, cache_mla_ckv, cache_mla_krope, state_ret, c, c_ctx, w_ada, b_ada, norm_g, w_in_ab, q_norm_g, w_uq, kv_norm_g, w_ukv, ret_log_decay, w_o_ab, w_in_c, ln_g_c, ln_b_c, w_s_c, b_s_c, w_out_c, w_ffn_gate, w_ffn_up, w_ffn_down, final_norm_g):
    batch, seq, d = x_prompt.shape
    dec_batch, dec_seq, _ = x_sample.shape
    depth = w_ada.shape[0]
    assert depth % 2 == 0
    past = cache_mla_ckv.shape[2]
    groups = [_Group("p", batch, seq, 0, positional=False), _Group("s", dec_batch, dec_seq, 1, positional=True)]
    ffn_groups = [grp.with_tile(FFN_ROW_TILE) for grp in groups]
    xs = [x_prompt.reshape(batch * seq, d), x_sample.reshape(dec_batch * dec_seq, d)]

    cond8 = jnp.zeros((8, d), F32).at[0].set(c_ctx).at[1:1 + dec_batch].set(c)
    mods = _ada_mods(cond8, w_ada, b_ada)

    table = _rope_table(dec_seq, ROW_TILE)
    cache_kr_padded = jnp.pad(cache_mla_krope, ((0, 0), (0, 0), (0, 0), (MLA_NOPE, LANE - MLA_NOPE - MLA_ROPE)))
    n_ab = w_in_ab.shape[0]
    gw = w_out_c.shape[1] // GM_GROUPS
    w = dict(
        norm_g=norm_g.reshape(depth, 2, 1, d), final_g=final_norm_g.reshape(1, d),
        ffn_gate=w_ffn_gate.astype(BF16), ffn_up=w_ffn_up.astype(BF16), ffn_down=w_ffn_down.astype(BF16),
        q_norm_g=q_norm_g.reshape(n_ab, 1, -1), kv_norm_g=kv_norm_g.reshape(n_ab, 1, -1), o_ab=w_o_ab.astype(BF16),
        in_c=w_in_c.astype(BF16), out_c=w_out_c.astype(BF16), s_c=w_s_c.astype(BF16),
        ln_g_c=ln_g_c[:, None, :], ln_b_c=ln_b_c[:, None, :],
        bs_c=jnp.broadcast_to(b_s_c[:, :, :, None], b_s_c.shape + (gw,)))
    ld = jnp.broadcast_to(ret_log_decay[:, :, :, None, None], ret_log_decay.shape + (CHUNK, CHUNK))

    ckv_out, kr_out, finals = [], [], None
    for l in range(depth):
        j = l // 2
        last = l == depth - 1
        if l % 2 == 0:
            win, wuq = _layout_w_in(w_in_ab[j]), _layout_w_uq(w_uq[j])
            wuk, wvt = _layout_w_ukv(w_ukv[j])
            kc, vtc = _ctxkv(j, cache_mla_ckv, cache_kr_padded, wuk, wvt)
            for gi, grp in enumerate(groups):
                prompt = gi == 0
                outs = _abproj(grp, l, j, xs[gi], mods, w, table, win, wuq, wuk, wvt, with_cache=prompt)
                q, k, vt, rq, rk, rv, rg = outs[:7]
                if prompt:
                    oat = _attention(f"attn_p_{l}", q, k, vt, grp.n_seq, grp.seq_len)
                    ret, finals = _retention(f"ret_p_{l}", j, ld, rq, rk, rv, rg, grp.n_seq, grp.seq_len,
                                             want_final=True, finals=finals)
                    ckv_out.append(outs[7].reshape(batch, seq, KV_LORA))
                    kr_out.append(outs[8][:, MLA_NOPE:MLA_NOPE + MLA_ROPE].reshape(batch, seq, MLA_ROPE))
                else:
                    oat = _attention(f"attn_s_{l}", q, k, vt, grp.n_seq, grp.seq_len, ctx=(kc, vtc, past))
                    ret = _retention(f"ret_s_{l}", j, ld, rq, rk, rv, rg, grp.n_seq, grp.seq_len, s0=state_ret)
                xs[gi] = _merge_ffn(ffn_groups[gi], l, j, xs[gi], mods, oat, ret, w)
        else:
            for gi, grp in enumerate(groups):
                x1 = _cmix(grp, l, j, xs[gi], mods, w)
                xs[gi] = _ffn(ffn_groups[gi], l, x1, mods, w, final_norm=last)

    y_prompt = xs[0].reshape(batch, seq, d)
    y_sample = xs[1].reshape(dec_batch, dec_seq, d)
    return (y_prompt, y_sample, jnp.stack(ckv_out, axis=1), jnp.stack(kr_out, axis=1), finals)
```
